```python
import math
import jax, jax.numpy as jnp
from jax import lax
import numpy as np

D_MODEL = 1024
BATCH = 2
SEQ = 16384
DEPTH = 4

GRID_W = 64
CTX_LEN = 256
HEAD_DIM = 64
ROPE_THETA = 10000.0
Q_BLOCK = 128
EPS = 1e-6
N_MOD = 6
DIFF_HEADS = 4
DIFF_QK_DIM = HEAD_DIM
DIFF_V_DIM = 2 * HEAD_DIM
GQA_Q_HEADS = 8
GQA_KV_HEADS = 2
GQA_GROUP = GQA_Q_HEADS // GQA_KV_HEADS
ATTN_IN = DIFF_HEADS * (4 * DIFF_QK_DIM + DIFF_V_DIM) + (GQA_Q_HEADS + 2 * GQA_KV_HEADS) * HEAD_DIM
ATTN_OUT = DIFF_HEADS * DIFF_V_DIM + GQA_Q_HEADS * HEAD_DIM
SGU_DIM = D_MODEL
SGU_GROUPS = 4
SGU_CHUNK = 128
FFN_DIM = 2816
CONV_W = 3
N_EVEN = (DEPTH + 1) // 2
N_ODD = DEPTH // 2

kernel_name = "hybrid_diffattn_gqa_sgu_convffn_dit"


def rms_norm(x, g):
    xf = x.astype(jnp.float32)
    y = xf * lax.rsqrt(jnp.mean(xf * xf, axis=-1, keepdims=True) + EPS)
    return (y * g.astype(jnp.float32)).astype(x.dtype)


def modulate(h, shift, scale):
    return h * (1 + scale) + shift


def axial_rope_tables(rows):
    n_freq = HEAD_DIM // 4
    inv = ROPE_THETA ** (-jnp.arange(n_freq, dtype=jnp.float32) / n_freq)
    row = jnp.repeat(jnp.arange(rows, dtype=jnp.float32), GRID_W)
    col = jnp.tile(jnp.arange(GRID_W, dtype=jnp.float32), rows)
    ang = jnp.concatenate([row[:, None] * inv, col[:, None] * inv], axis=-1)
    return jnp.cos(ang), jnp.sin(ang)


def apply_rope(x, cos, sin):
    xf = x.astype(jnp.float32).reshape(x.shape[:-1] + (x.shape[-1] // 2, 2))
    x0, x1 = xf[..., 0], xf[..., 1]
    c = cos[None, :, None, :]
    s = sin[None, :, None, :]
    out = jnp.stack([x0 * c - x1 * s, x0 * s + x1 * c], axis=-1).reshape(x.shape)
    return out.astype(x.dtype)


def attn_project(h, w_in, q_norm, k_norm, rope):
    B, S, _ = h.shape
    qk_a = DIFF_HEADS * DIFF_QK_DIM
    sizes = [qk_a, qk_a, qk_a, qk_a, DIFF_HEADS * DIFF_V_DIM,
             GQA_Q_HEADS * HEAD_DIM, GQA_KV_HEADS * HEAD_DIM, GQA_KV_HEADS * HEAD_DIM]
    splits = [int(v) for v in np.cumsum(sizes)[:-1]]
    q1, q2, k1, k2, va, qb, kb, vb = jnp.split(h @ w_in, splits, axis=-1)
    heads = lambda t, n: t.reshape(B, S, n, -1)
    q1, q2, k1, k2, va = (heads(t, DIFF_HEADS) for t in (q1, q2, k1, k2, va))
    qb = rms_norm(heads(qb, GQA_Q_HEADS), q_norm)
    kb = rms_norm(heads(kb, GQA_KV_HEADS), k_norm)
    vb = heads(vb, GQA_KV_HEADS)
    if rope is not None:
        cos, sin = rope
        q1, q2, k1, k2, qb, kb = (apply_rope(t, cos, sin) for t in (q1, q2, k1, k2, qb, kb))
    return q1, q2, qb, k1, k2, va, kb, vb


def diff_attend(q1, q2, k1, k2, v, lam):
    scale = DIFF_QK_DIM ** -0.5
    p1 = jax.nn.softmax(jnp.einsum('bqhd,bkhd->bhqk', q1, k1).astype(jnp.float32) * scale, axis=-1)
    p2 = jax.nn.softmax(jnp.einsum('bqhd,bkhd->bhqk', q2, k2).astype(jnp.float32) * scale, axis=-1)
    p = (p1 - lam * p2).astype(v.dtype)
    return jnp.einsum('bhqk,bkhv->bqhv', p, v)


def gqa_attend(q, k, v):
    B, Q, H, d = q.shape
    qg = q.reshape(B, Q, GQA_KV_HEADS, GQA_GROUP, d)
    s = jnp.einsum('bqgrd,bkgd->bgrqk', qg, k).astype(jnp.float32) * (d ** -0.5)
    p = jax.nn.softmax(s, axis=-1).astype(v.dtype)
    return jnp.einsum('bgrqk,bkgd->bqgrd', p, v).reshape(B, Q, H, d)


def attn_heads(q1, q2, qb, k1, k2, va, kb, vb, lam, lam_init, subln):
    B, Q = q1.shape[:2]
    oa = rms_norm(diff_attend(q1, q2, k1, k2, va, lam), subln) * (1 - lam_init)
    ob = gqa_attend(qb, kb, vb)
    return jnp.concatenate([oa.reshape(B, Q, -1), ob.reshape(B, Q, -1)], axis=-1)


def sweep_query_blocks(fn, qs):
    B, S = qs[0].shape[:2]
    nb = S // Q_BLOCK
    blocks = tuple(jnp.moveaxis(q.reshape((B, nb, Q_BLOCK) + q.shape[2:]), 1, 0) for q in qs)
    out = lax.map(lambda qb: fn(*qb), blocks)
    return jnp.moveaxis(out, 0, 1).reshape(B, S, out.shape[-1])


def attention_mixer(h_lat, h_ctx, w_in, w_out, lq1, lk1, lq2, lk2, subln, q_norm, k_norm,
                    lam_init, rope, with_ctx_out):
    lat = attn_project(h_lat, w_in, q_norm, k_norm, rope)
    ctx = attn_project(h_ctx, w_in, q_norm, k_norm, None)
    lam = (jnp.exp(jnp.sum(lq1.astype(jnp.float32) * lk1.astype(jnp.float32)))
           - jnp.exp(jnp.sum(lq2.astype(jnp.float32) * lk2.astype(jnp.float32))) + lam_init)
    K1, K2, VA, KB, VB = (jnp.concatenate([ctx[i], lat[i]], axis=1) for i in range(3, 8))
    lat_fn = lambda q1b, q2b, qbb: attn_heads(q1b, q2b, qbb, K1, K2, VA, KB, VB, lam, lam_init, subln)
    y_lat = sweep_query_blocks(lat_fn, lat[:3]) @ w_out
    y_ctx = None
    if with_ctx_out:
        y_ctx = attn_heads(*ctx, lam, lam_init, subln) @ w_out
    return y_lat, y_ctx


def sgu_mixer(h, w_in, v_norm, w_s, b_s, w_out):
    B, S, _ = h.shape
    z = jax.nn.gelu(h @ w_in, approximate=False)
    u, v = jnp.split(z, 2, axis=-1)
    v = rms_norm(v, v_norm)
    n = S // SGU_CHUNK
    vg = v.reshape(B, n, SGU_CHUNK, SGU_GROUPS, SGU_DIM // SGU_GROUPS)
    mixed = jnp.einsum('gpq,bnqgc->bnpgc', w_s, vg) + b_s.T[:, :, None]
    return (u * mixed.reshape(B, S, SGU_DIM)) @ w_out


def conv_ffn(h, w_up, conv_w, conv_b, w_down):
    S = h.shape[1]
    z = h @ w_up
    zp = jnp.pad(z, ((0, 0), (1, 1), (0, 0)))
    z = conv_w[0] * zp[:, :S] + conv_w[1] * zp[:, 1:S + 1] + conv_w[2] * zp[:, 2:] + conv_b
    g, u = jnp.split(z, 2, axis=-1)
    return (jax.nn.silu(g) * u) @ w_down


def setup_inputs(seed: int = 0) -> dict:
    key = jax.random.key(seed)
    ks = iter(jax.random.split(key, 32))
    D = D_MODEL
    nrm = lambda shape, s: jax.random.normal(next(ks), shape, jnp.float32) * s
    return {
        "x": nrm((BATCH, SEQ, D), 1.0),
        "c": nrm((BATCH, D), 1.0),
        "ctx": nrm((BATCH, CTX_LEN, D), 1.0),
        "c_ctx": nrm((D,), 1.0),
        "ada_w": nrm((DEPTH, D, N_MOD * D), 0.5 * D ** -0.5),
        "ada_b": nrm((DEPTH, N_MOD * D), 0.02),
        "mix_norm": 1.0 + nrm((DEPTH, D), 0.1),
        "ffn_norm": 1.0 + nrm((DEPTH, D), 0.1),
        "final_norm": 1.0 + nrm((D,), 0.1),
        "attn_w_in": nrm((N_EVEN, D, ATTN_IN), D ** -0.5),
        "attn_w_out": nrm((N_EVEN, ATTN_OUT, D), ATTN_OUT ** -0.5),
        "diff_lq1": nrm((N_EVEN, DIFF_QK_DIM), 0.1),
        "diff_lk1": nrm((N_EVEN, DIFF_QK_DIM), 0.1),
        "diff_lq2": nrm((N_EVEN, DIFF_QK_DIM), 0.1),
        "diff_lk2": nrm((N_EVEN, DIFF_QK_DIM), 0.1),
        "diff_subln": 1.0 + nrm((N_EVEN, DIFF_V_DIM), 0.1),
        "gqa_q_norm": 1.0 + nrm((N_EVEN, HEAD_DIM), 0.1),
        "gqa_k_norm": 1.0 + nrm((N_EVEN, HEAD_DIM), 0.1),
        "sgu_w_in": nrm((N_ODD, D, 2 * SGU_DIM), D ** -0.5),
        "sgu_v_norm": 1.0 + nrm((N_ODD, SGU_DIM), 0.1),
        "sgu_w_s": nrm((N_ODD, SGU_GROUPS, SGU_CHUNK, SGU_CHUNK), SGU_CHUNK ** -0.5),
        "sgu_b_s": 1.0 + nrm((N_ODD, SGU_GROUPS, SGU_CHUNK), 0.1),
        "sgu_w_out": nrm((N_ODD, SGU_DIM, D), SGU_DIM ** -0.5),
        "ffn_w_up": nrm((DEPTH, D, 2 * FFN_DIM), D ** -0.5),
        "ffn_conv_w": nrm((DEPTH, CONV_W, 2 * FFN_DIM), CONV_W ** -0.5),
        "ffn_conv_b": nrm((DEPTH, 2 * FFN_DIM), 0.02),
        "ffn_w_down": nrm((DEPTH, FFN_DIM, D), FFN_DIM ** -0.5),
    }


def reference(x, c, ctx, c_ctx, ada_w, ada_b, mix_norm, ffn_norm, final_norm,
              attn_w_in, attn_w_out, diff_lq1, diff_lk1, diff_lq2, diff_lk2, diff_subln,
              gqa_q_norm, gqa_k_norm, sgu_w_in, sgu_v_norm, sgu_w_s, sgu_b_s, sgu_w_out,
              ffn_w_up, ffn_conv_w, ffn_conv_b, ffn_w_down):
    rows = x.shape[1] // GRID_W
    rope = axial_rope_tables(rows)
    last_attn = (DEPTH - 1) // 2 * 2
    s_c = jax.nn.silu(c)
    s_cc = jax.nn.silu(c_ctx)
    for l in range(DEPTH):
        i = l // 2
        is_attn = l % 2 == 0
        update_ctx = l < last_attn
        m_lat = (s_c @ ada_w[l] + ada_b[l])[:, None, :]
        sh1, sc1, g1, sh2, sc2, g2 = jnp.split(m_lat, N_MOD, axis=-1)
        h_lat = modulate(rms_norm(x, mix_norm[l]), sh1, sc1)
        if is_attn or update_ctx:
            m_ctx = s_cc @ ada_w[l] + ada_b[l]
            csh1, csc1, cg1, csh2, csc2, cg2 = jnp.split(m_ctx, N_MOD, axis=-1)
            h_ctx = modulate(rms_norm(ctx, mix_norm[l]), csh1, csc1)
        if is_attn:
            lam_init = 0.8 - 0.6 * math.exp(-0.3 * l)
            y_lat, y_ctx = attention_mixer(h_lat, h_ctx, attn_w_in[i], attn_w_out[i],
                                           diff_lq1[i], diff_lk1[i], diff_lq2[i], diff_lk2[i],
                                           diff_subln[i], gqa_q_norm[i], gqa_k_norm[i],
                                           lam_init, rope, update_ctx)
        else:
            y_lat = sgu_mixer(h_lat, sgu_w_in[i], sgu_v_norm[i], sgu_w_s[i], sgu_b_s[i], sgu_w_out[i])
            if update_ctx:
                y_ctx = sgu_mixer(h_ctx, sgu_w_in[i], sgu_v_norm[i], sgu_w_s[i], sgu_b_s[i], sgu_w_out[i])
        x = x + g1 * y_lat
        x = x + g2 * conv_ffn(modulate(rms_norm(x, ffn_norm[l]), sh2, sc2),
                              ffn_w_up[l], ffn_conv_w[l], ffn_conv_b[l], ffn_w_down[l])
        if update_ctx:
            ctx = ctx + cg1 * y_ctx
            ctx = ctx + cg2 * conv_ffn(modulate(rms_norm(ctx, ffn_norm[l]), csh2, csc2),
                                       ffn_w_up[l], ffn_conv_w[l], ffn_conv_b[l], ffn_w_down[l])
    return rms_norm(x, final_norm)
```

```python
import functools
import math

import jax
import jax.numpy as jnp
from jax import lax
from jax.experimental import pallas as pl
from jax.experimental.pallas import tpu as pltpu

F32 = jnp.float32
BF16 = jnp.bfloat16

EPS = 1e-6
HEAD = 64
N_DIFF = 4
DIFF_V = 2 * HEAD
N_GQ = 8
N_GKV = 2
GQ_GROUP = N_GQ // N_GKV
ROPE_THETA = 10000.0
GRID_COLS = 64
N_MOD = 6
SGU_CHUNK = 128
SGU_GROUPS = 4
CONV_HALO = 8
VMEM_LIMIT = 56 * 1024 * 1024

QK_A = N_DIFF * HEAD
C_Q1, C_Q2, C_K1, C_K2 = 0, QK_A, 2 * QK_A, 3 * QK_A
C_VA = 4 * QK_A
C_QB = C_VA + N_DIFF * DIFF_V
C_KB = C_QB + N_GQ * HEAD
C_VB = C_KB + N_GKV * HEAD
C_END = C_VB + N_GKV * HEAD
R_QK = C_END
R_QB = R_QK + 4 * QK_A
R_KB = R_QB + N_GQ * HEAD
R_END = R_KB + N_GKV * HEAD


def _cparams(n_axes):
    return pltpu.CompilerParams(dimension_semantics=("arbitrary",) * n_axes,
                                vmem_limit_bytes=VMEM_LIMIT)


def _const_spec(shape):
    nd = len(shape)
    return pl.BlockSpec(shape, lambda *_: (0,) * nd, pipeline_mode=pl.Buffered(1))


def _rms(xf, g):
    ms = jnp.mean(xf * xf, axis=-1, keepdims=True)
    return xf * lax.rsqrt(ms + EPS) * g


def _sigmoid(x):
    return 1.0 / (1.0 + jnp.exp(-x))


def _mod_kernel(c_ref, w_ref, b_ref, o_ref):
    c = c_ref[...]
    s = c * _sigmoid(c)
    o_ref[...] = jnp.dot(s, w_ref[...], precision=lax.Precision.HIGHEST,
                         preferred_element_type=F32) + b_ref[...]


def _mod_vectors(cvec, ada_w, ada_b):
    depth, d, n = ada_w.shape
    rows = cvec.shape[0]
    tn = 1536
    return pl.pallas_call(
        _mod_kernel,
        out_shape=jax.ShapeDtypeStruct((depth, rows, n), F32),
        grid=(depth, n // tn),
        in_specs=[pl.BlockSpec((rows, d), lambda l, j: (0, 0)),
                  pl.BlockSpec((None, d, tn), lambda l, j: (l, 0, j)),
                  pl.BlockSpec((None, 1, tn), lambda l, j: (l, 0, j))],
        out_specs=pl.BlockSpec((None, rows, tn), lambda l, j: (l, 0, j)),
        compiler_params=_cparams(2),
        name="mod_vectors",
    )(cvec, ada_w, ada_b.reshape(depth, 1, n))


def _group_sumsq(a, e):
    sq = a * a
    hi = sq.astype(BF16)
    lo = (sq - hi.astype(F32)).astype(BF16)
    return (jnp.dot(hi, e, preferred_element_type=F32) + jnp.dot(lo, e, preferred_element_type=F32))


def _attn_proj_kernel(x_ref, sh_ref, sc_ref, nw_ref, w_ref, cos_ref, sin_ref,
                      gq_ref, gqs_ref, gk_ref, gks_ref, e_ref,
                      qd_ref, kd_ref, va_ref, qg_ref, kg_ref, vg_ref):
    x = x_ref[...]
    h = (_rms(x, nw_ref[...]) * (1.0 + sc_ref[...]) + sh_ref[...]).astype(BF16)
    a = jnp.dot(h, w_ref[...], preferred_element_type=F32)
    cos = cos_ref[...]
    sin = sin_ref[...]
    scale = HEAD ** -0.5

    for j in range(4):
        r = (a[:, C_Q1 + QK_A * j:C_Q1 + QK_A * (j + 1)] * cos
             + a[:, R_QK + QK_A * j:R_QK + QK_A * (j + 1)] * sin)
        if j < 2:
            r = r * scale
        r = r.astype(BF16)
        dst = qd_ref if j < 2 else kd_ref
        for hh in range(N_DIFF):
            dst[(j % 2) * N_DIFF + hh] = r[:, HEAD * hh:HEAD * (hh + 1)]
    va = a[:, C_VA:C_QB].astype(BF16)
    for hh in range(N_DIFF):
        va_ref[hh] = va[:, DIFF_V * hh:DIFF_V * (hh + 1)]

    e = e_ref[...]
    cos2 = jnp.concatenate([cos, cos], axis=1)
    sin2 = jnp.concatenate([sin, sin], axis=1)
    qb = a[:, C_QB:C_KB]
    rq = lax.rsqrt(_group_sumsq(qb, e) * (1.0 / HEAD) + EPS)
    q = (rq * (qb * (gq_ref[...] * cos2) + a[:, R_QB:R_KB] * (gqs_ref[...] * sin2)) * scale).astype(BF16)
    for hh in range(N_GQ):
        qg_ref[hh] = q[:, HEAD * hh:HEAD * (hh + 1)]
    nk = N_GKV * HEAD
    kb = a[:, C_KB:C_VB]
    rk = lax.rsqrt(_group_sumsq(kb, e[:nk, :nk]) * (1.0 / HEAD) + EPS)
    k = (rk * (kb * (gk_ref[...] * cos[:, :nk]) + a[:, R_KB:R_END] * (gks_ref[...] * sin[:, :nk]))).astype(BF16)
    vb = a[:, C_VB:C_END].astype(BF16)
    for hh in range(N_GKV):
        kg_ref[hh] = k[:, HEAD * hh:HEAD * (hh + 1)]
        vg_ref[hh] = vb[:, HEAD * hh:HEAD * (hh + 1)]


def _attn_proj(x, sh, sc, nw, w_cat, cos, sin, gq, gqs, gk, gks, e, tile):
    b, s, d = x.shape
    t = min(tile, s)
    hd = lambda n, w: jax.ShapeDtypeStruct((b, n, s, w), BF16)
    hspec = lambda n, w: pl.BlockSpec((None, n, t, w), lambda bi, i: (bi, 0, i, 0))
    vec = pl.BlockSpec((None, 1, d), lambda bi, i: (bi, 0, 0))
    return pl.pallas_call(
        _attn_proj_kernel,
        out_shape=(hd(2 * N_DIFF, HEAD), hd(2 * N_DIFF, HEAD), hd(N_DIFF, DIFF_V),
                   hd(N_GQ, HEAD), hd(N_GKV, HEAD), hd(N_GKV, HEAD)),
        grid=(b, s // t),
        in_specs=[pl.BlockSpec((None, t, d), lambda bi, i: (bi, i, 0)), vec, vec,
                  _const_spec((1, d)), _const_spec(w_cat.shape),
                  pl.BlockSpec((t, 4 * HEAD), lambda bi, i: (i, 0)),
                  pl.BlockSpec((t, 4 * HEAD), lambda bi, i: (i, 0)),
                  _const_spec(gq.shape), _const_spec(gqs.shape),
                  _const_spec(gk.shape), _const_spec(gks.shape), _const_spec(e.shape)],
        out_specs=(hspec(2 * N_DIFF, HEAD), hspec(2 * N_DIFF, HEAD), hspec(N_DIFF, DIFF_V),
                   hspec(N_GQ, HEAD), hspec(N_GKV, HEAD), hspec(N_GKV, HEAD)),
        compiler_params=_cparams(2),
        name="attn_proj",
    )(x, sh, sc, nw, w_cat, cos, sin, gq, gqs, gk, gks, e)


def _flash_kernel(q_ref, k_ref, v_ref, o_ref, m_scr, l_scr, acc_scr, *, group, tq, tk, nk):
    m_rows = group * tq
    q = q_ref[...].reshape(m_rows, HEAD)
    m_scr[...] = jnp.full(m_scr.shape, -jnp.inf, F32)
    l_scr[...] = jnp.zeros(l_scr.shape, F32)
    acc_scr[...] = jnp.zeros(acc_scr.shape, F32)

    def body(j, carry):
        start = pl.multiple_of(j * tk, tk)
        k = k_ref[pl.ds(start, tk), :]
        v = v_ref[pl.ds(start, tk), :]
        s = lax.dot_general(q, k, (((1,), (1,)), ((), ())), preferred_element_type=F32)
        m_old = m_scr[...]
        m_new = jnp.maximum(m_old, jnp.max(s, axis=1, keepdims=True))
        alpha = jnp.exp(m_old - m_new)
        p = jnp.exp(s - m_new)
        l_scr[...] = alpha * l_scr[...] + jnp.sum(p, axis=1, keepdims=True)
        acc_scr[...] = alpha * acc_scr[...] + jnp.dot(p.astype(BF16), v, preferred_element_type=F32)
        m_scr[...] = m_new
        return carry

    lax.fori_loop(0, nk, body, 0)
    o = acc_scr[...] / l_scr[...]
    if group > 1:
        o = jnp.concatenate([o[g * tq:(g + 1) * tq] for g in range(group)], axis=1)
    o_ref[...] = o.astype(o_ref.dtype)


def _pick_tk(sk, cap):
    best = 128
    for t in range(128, min(sk, cap) + 1, 128):
        if sk % t == 0:
            best = t
    return best


def _flash(q, k, v, group, tq, out_dtype, tk_cap=1280):
    b, hq, s, _ = q.shape
    hk, sk = k.shape[1], k.shape[2]
    hv, dv = v.shape[1], v.shape[3]
    tq = min(tq, s)
    tk = _pick_tk(sk, tk_cap)
    m_rows = group * tq
    kern = functools.partial(_flash_kernel, group=group, tq=tq, tk=tk, nk=sk // tk)
    return pl.pallas_call(
        kern,
        out_shape=jax.ShapeDtypeStruct((b, s, hq * dv), out_dtype),
        grid=(b, hk, s // tq),
        in_specs=[pl.BlockSpec((None, group, tq, HEAD), lambda bi, h, i: (bi, h, i, 0)),
                  pl.BlockSpec((None, None, sk, HEAD), lambda bi, h, i: (bi, h, 0, 0)),
                  pl.BlockSpec((None, None, sk, dv), lambda bi, h, i: (bi, h % hv, 0, 0))],
        out_specs=pl.BlockSpec((None, tq, group * dv), lambda bi, h, i: (bi, i, h)),
        scratch_shapes=[pltpu.VMEM((m_rows, 1), F32), pltpu.VMEM((m_rows, 1), F32),
                        pltpu.VMEM((m_rows, dv), F32)],
        compiler_params=_cparams(3),
        name="flash_g%d" % group,
    )(q, k, v)


def _attn_out_kernel(od_ref, og_ref, lq1_ref, lk1_ref, lq2_ref, lk2_ref, subln_ref, w_ref,
                     x_ref, g1_ref, o_ref, *, lam_init):
    lam = (jnp.exp(jnp.sum(lq1_ref[...] * lk1_ref[...], axis=-1, keepdims=True))
           - jnp.exp(jnp.sum(lq2_ref[...] * lk2_ref[...], axis=-1, keepdims=True)) + lam_init)
    od = od_ref[...]
    half = N_DIFF * DIFF_V
    parts = []
    for hh in range(N_DIFF):
        dh = od[:, DIFF_V * hh:DIFF_V * (hh + 1)] - lam * od[:, half + DIFF_V * hh:half + DIFF_V * (hh + 1)]
        parts.append(_rms(dh, subln_ref[...]) * (1.0 - lam_init))
    att = jnp.concatenate([p.astype(BF16) for p in parts] + [og_ref[...]], axis=1)
    y = jnp.dot(att, w_ref[...], preferred_element_type=F32)
    o_ref[...] = x_ref[...] + g1_ref[...] * y


def _attn_out(od, og, lq1, lk1, lq2, lk2, subln, w_out, x, g1, lam_init, tile):
    b, s, d = x.shape
    t = min(tile, s)
    vec = pl.BlockSpec((None, 1, d), lambda bi, i: (bi, 0, 0))
    row = lambda a: pl.BlockSpec((None, t, a.shape[2]), lambda bi, i: (bi, i, 0))
    return pl.pallas_call(
        functools.partial(_attn_out_kernel, lam_init=lam_init),
        out_shape=jax.ShapeDtypeStruct(x.shape, F32),
        grid=(b, s // t),
        in_specs=[row(od), row(og), _const_spec(lq1.shape), _const_spec(lk1.shape),
                  _const_spec(lq2.shape), _const_spec(lk2.shape), _const_spec(subln.shape),
                  _const_spec(w_out.shape), row(x), vec],
        out_specs=row(x),
        compiler_params=_cparams(2),
        name="attn_out",
    )(od, og, lq1, lk1, lq2, lk2, subln, w_out, x, g1)


def _sgu_kernel(x_ref, sh_ref, sc_ref, g1_ref, nw_ref, win_ref, vn_ref, ws_ref, bs_ref, wout_ref, o_ref):
    x = x_ref[...]
    t, d = x.shape
    h = (_rms(x, nw_ref[...]) * (1.0 + sc_ref[...]) + sh_ref[...]).astype(BF16)
    z = jnp.dot(h, win_ref[...], preferred_element_type=F32)
    z = 0.5 * z * (1.0 + lax.erf(z * (2.0 ** -0.5)))
    sd = z.shape[1] // 2
    u = z[:, :sd]
    v = _rms(z[:, sd:], vn_ref[...]).astype(BF16)
    gw = sd // SGU_GROUPS
    rows = []
    for n in range(t // SGU_CHUNK):
        cols = []
        for g in range(SGU_GROUPS):
            vg = v[SGU_CHUNK * n:SGU_CHUNK * (n + 1), gw * g:gw * (g + 1)]
            cols.append(jnp.dot(ws_ref[g], vg, preferred_element_type=F32))
        rows.append(jnp.concatenate(cols, axis=1) + bs_ref[...])
    mixed = jnp.concatenate(rows, axis=0) if len(rows) > 1 else rows[0]
    y = jnp.dot((u * mixed).astype(BF16), wout_ref[...], preferred_element_type=F32)
    o_ref[...] = x + g1_ref[...] * y


def _sgu(x, sh, sc, g1, nw, w_in, vn, w_s, bs_full, w_out, tile):
    b, s, d = x.shape
    t = min(tile, s)
    vec = pl.BlockSpec((None, 1, d), lambda bi, i: (bi, 0, 0))
    row = pl.BlockSpec((None, t, d), lambda bi, i: (bi, i, 0))
    return pl.pallas_call(
        _sgu_kernel,
        out_shape=jax.ShapeDtypeStruct(x.shape, F32),
        grid=(b, s // t),
        in_specs=[row, vec, vec, vec, _const_spec(nw.shape), _const_spec(w_in.shape),
                  _const_spec(vn.shape), _const_spec(w_s.shape), _const_spec(bs_full.shape),
                  _const_spec(w_out.shape)],
        out_specs=row,
        compiler_params=_cparams(2),
        name="sgu",
    )(x, sh, sc, g1, nw, w_in, vn, w_s, bs_full, w_out)


def _ffn_kernel(xp_ref, x_ref, xn_ref, sh_ref, sc_ref, g2_ref, nw_ref, wup_ref, cw_ref, cb_ref,
                wdn_ref, fn_ref, o_ref, z_scr, *, final_norm):
    i = pl.program_id(1)
    last = pl.num_programs(1) - 1
    x = x_ref[...]
    t = x.shape[0]
    xa = jnp.concatenate([xp_ref[...], x, xn_ref[...]], axis=0)
    h = _rms(xa, nw_ref[...]) * (1.0 + sc_ref[...]) + sh_ref[...]
    r = lax.broadcasted_iota(jnp.int32, (t + 2 * CONV_HALO, 1), 0)
    outside = ((r < CONV_HALO) & (i == 0)) | ((r >= t + CONV_HALO) & (i == last))
    h = jnp.where(outside, 0.0, h).astype(BF16)
    z_scr[...] = jnp.dot(h, wup_ref[...], preferred_element_type=F32)
    cw = cw_ref[...]
    zc = (cw[0:1] * z_scr[pl.ds(CONV_HALO - 1, t), :] + cw[1:2] * z_scr[pl.ds(CONV_HALO, t), :]
          + cw[2:3] * z_scr[pl.ds(CONV_HALO + 1, t), :] + cb_ref[...])
    f = zc.shape[1] // 2
    g = zc[:, :f]
    act = (g * _sigmoid(g) * zc[:, f:]).astype(BF16)
    y = jnp.dot(act, wdn_ref[...], preferred_element_type=F32)
    out = x + g2_ref[...] * y
    if final_norm:
        out = _rms(out, fn_ref[...])
    o_ref[...] = out


def _ffn(x, sh, sc, g2, nw, w_up, conv_w, conv_b, w_dn, fn, final_norm, tile):
    b, s, d = x.shape
    t = min(tile, s)
    tb = t // CONV_HALO
    nhb = s // CONV_HALO
    vec = pl.BlockSpec((None, 1, d), lambda bi, i: (bi, 0, 0))
    row = pl.BlockSpec((None, t, d), lambda bi, i: (bi, i, 0))
    prev = pl.BlockSpec((None, CONV_HALO, d), lambda bi, i: (bi, jnp.maximum(i * tb - 1, 0), 0))
    nxt = pl.BlockSpec((None, CONV_HALO, d), lambda bi, i: (bi, jnp.minimum((i + 1) * tb, nhb - 1), 0))
    return pl.pallas_call(
        functools.partial(_ffn_kernel, final_norm=final_norm),
        out_shape=jax.ShapeDtypeStruct(x.shape, F32),
        grid=(b, s // t),
        in_specs=[prev, row, nxt, vec, vec, vec, _const_spec(nw.shape), _const_spec(w_up.shape),
                  _const_spec(conv_w.shape), _const_spec(conv_b.shape), _const_spec(w_dn.shape),
                  _const_spec(fn.shape)],
        out_specs=row,
        scratch_shapes=[pltpu.VMEM((t + 2 * CONV_HALO, w_up.shape[1]), F32)],
        compiler_params=_cparams(2),
        name="ffn",
    )(x, x, x, sh, sc, g2, nw, w_up, conv_w, conv_b, w_dn, fn)


def _pair_swap_cols(w):
    d, n = w.shape
    w2 = w.reshape(d, n // 2, 2)
    return jnp.stack([-w2[..., 1], w2[..., 0]], axis=-1).reshape(d, n)


def _pair_swap_vec(g):
    return g.reshape(-1, 2)[:, ::-1].reshape(-1)


def _rope_tables(rows):
    n_freq = HEAD // 4
    inv = ROPE_THETA ** (-jnp.arange(n_freq, dtype=F32) / n_freq)
    row = jnp.repeat(jnp.arange(rows, dtype=F32), GRID_COLS)
    col = jnp.tile(jnp.arange(GRID_COLS, dtype=F32), rows)
    ang = jnp.concatenate([row[:, None] * inv, col[:, None] * inv], axis=-1)
    expand = lambda t: jnp.tile(jnp.repeat(t, 2, axis=-1), (1, N_DIFF))
    return expand(jnp.cos(ang)), expand(jnp.sin(ang))


def kernel(x, c, ctx, c_ctx, ada_w, ada_b, mix_norm, ffn_norm, final_norm, attn_w_in, attn_w_out,
           diff_lq1, diff_lk1, diff_lq2, diff_lk2, diff_subln, gqa_q_norm, gqa_k_norm, sgu_w_in,
           sgu_v_norm, sgu_w_s, sgu_b_s, sgu_w_out, ffn_w_up, ffn_conv_w, ffn_conv_b, ffn_w_down):
    b, s, d = x.shape
    n_ctx = ctx.shape[1]
    depth = ada_w.shape[0]
    last_attn = (depth - 1) // 2 * 2

    rows_pad = -(-(b + 1) // 8) * 8
    cvec = jnp.zeros((rows_pad, d), F32).at[:b].set(c).at[b].set(c_ctx)
    mods = _mod_vectors(cvec, ada_w, ada_b)

    cos_lat, sin_lat = _rope_tables(s // GRID_COLS)
    cos_ctx = jnp.ones((n_ctx, 4 * HEAD), F32)
    sin_ctx = jnp.zeros((n_ctx, 4 * HEAD), F32)
    eye = jnp.repeat(jnp.repeat(jnp.eye(N_GQ, dtype=BF16), HEAD, axis=0), HEAD, axis=1)
    fn = final_norm.reshape(1, d)

    def split_mod(m):
        return [m[:, k * d:(k + 1) * d].reshape(b, 1, d) for k in range(N_MOD)]

    for l in range(depth):
        i = l // 2
        is_attn = l % 2 == 0
        update_ctx = l < last_attn
        sh1, sc1, g1, sh2, sc2, g2 = split_mod(mods[l, :b])
        need_ctx = is_attn or update_ctx
        if need_ctx:
            csh1, csc1, cg1, csh2, csc2, cg2 = split_mod(jnp.broadcast_to(mods[l, b], (b, N_MOD * d)))
        nw1 = mix_norm[l].reshape(1, d)
        nw2 = ffn_norm[l].reshape(1, d)

        if is_attn:
            lam_init = 0.8 - 0.6 * math.exp(-0.3 * l)
            w = attn_w_in[i]
            w_cat = jnp.concatenate(
                [w, _pair_swap_cols(w[:, C_Q1:C_VA]), _pair_swap_cols(w[:, C_QB:C_KB]),
                 _pair_swap_cols(w[:, C_KB:C_VB])], axis=1).astype(BF16)
            gq = jnp.tile(gqa_q_norm[i], N_GQ).reshape(1, -1)
            gqs = jnp.tile(_pair_swap_vec(gqa_q_norm[i]), N_GQ).reshape(1, -1)
            gk = jnp.tile(gqa_k_norm[i], N_GKV).reshape(1, -1)
            gks = jnp.tile(_pair_swap_vec(gqa_k_norm[i]), N_GKV).reshape(1, -1)
            w_out = attn_w_out[i].astype(BF16)
            vecs = [v[i].reshape(1, -1) for v in (diff_lq1, diff_lk1, diff_lq2, diff_lk2, diff_subln)]

            lat = _attn_proj(x, sh1, sc1, nw1, w_cat, cos_lat, sin_lat, gq, gqs, gk, gks, eye, 512)
            cpr = _attn_proj(ctx, csh1, csc1, nw1, w_cat, cos_ctx, sin_ctx, gq, gqs, gk, gks, eye, 256)
            cat = lambda k: jnp.concatenate([cpr[k], lat[k]], axis=2)
            od = _flash(lat[0], cat(1), cat(2), 1, 1024, F32)
            og = _flash(lat[3], cat(4), cat(5), GQ_GROUP, 256, BF16)
            x_new = _attn_out(od, og, *vecs, w_out, x, g1, lam_init, 512)
            if update_ctx:
                cod = _flash(cpr[0], cpr[1], cpr[2], 1, 1024, F32)
                cog = _flash(cpr[3], cpr[4], cpr[5], GQ_GROUP, 256, BF16)
                ctx = _attn_out(cod, cog, *vecs, w_out, ctx, cg1, lam_init, 256)
            x = x_new
        else:
            w_in = sgu_w_in[i].astype(BF16)
            w_s = sgu_w_s[i].astype(BF16)
            w_out = sgu_w_out[i].astype(BF16)
            vn = sgu_v_norm[i].reshape(1, -1)
            gw = sgu_w_out.shape[1] // SGU_GROUPS
            bs_full = jnp.repeat(sgu_b_s[i].T, gw, axis=1)
            x_new = _sgu(x, sh1, sc1, g1, nw1, w_in, vn, w_s, bs_full, w_out, 256)
            if update_ctx:
                ctx = _sgu(ctx, csh1, csc1, cg1, nw1, w_in, vn, w_s, bs_full, w_out, 256)
            x = x_new

        w_up = ffn_w_up[l].astype(BF16)
        w_dn = ffn_w_down[l].astype(BF16)
        cb = ffn_conv_b[l].reshape(1, -1)
        x = _ffn(x, sh2, sc2, g2, nw2, w_up, ffn_conv_w[l], cb, w_dn, fn, l == depth - 1, 256)
        if update_ctx:
            ctx = _ffn(ctx, csh2, csc2, cg2, nw2, w_up, ffn_conv_w[l], cb, w_dn, fn, False, 256)
    return x
```

```python
import functools
import math

import jax
import jax.numpy as jnp
from jax import lax
from jax.experimental import pallas as pl
from jax.experimental.pallas import tpu as pltpu

F32 = jnp.float32
BF16 = jnp.bfloat16

EPS = 1e-6
HEAD = 64
N_DIFF = 4
DIFF_V = 2 * HEAD
N_GQ = 8
N_GKV = 2
GQ_GROUP = N_GQ // N_GKV
ROPE_THETA = 10000.0
GRID_COLS = 64
N_MOD = 6
SGU_CHUNK = 128
SGU_GROUPS = 4
CONV_HALO = 8
V_EXT = 16
KV_TILE = 512
QK_SCALE = HEAD ** -0.5 * math.log2(math.e)
VMEM_LIMIT = 56 * 1024 * 1024

QK_A = N_DIFF * HEAD
C_Q1, C_Q2, C_K1, C_K2 = 0, QK_A, 2 * QK_A, 3 * QK_A
C_VA = 4 * QK_A
C_QB = C_VA + N_DIFF * DIFF_V
C_KB = C_QB + N_GQ * HEAD
C_VB = C_KB + N_GKV * HEAD
C_END = C_VB + N_GKV * HEAD
R_QK = C_END
R_QB = R_QK + 4 * QK_A
R_KB = R_QB + N_GQ * HEAD
R_END = R_KB + N_GKV * HEAD


def _cparams(n_axes):
    return pltpu.CompilerParams(dimension_semantics=("arbitrary",) * n_axes,
                                vmem_limit_bytes=VMEM_LIMIT)


def _const_spec(shape):
    nd = len(shape)
    return pl.BlockSpec(shape, lambda *_: (0,) * nd, pipeline_mode=pl.Buffered(1))


def _rms(xf, g):
    ms = jnp.mean(xf * xf, axis=-1, keepdims=True)
    return xf * lax.rsqrt(ms + EPS) * g


def _sigmoid(x):
    return 1.0 / (1.0 + jnp.exp(-x))


def _mod_kernel(c_ref, w_ref, b_ref, o_ref):
    c = c_ref[...]
    s = c * _sigmoid(c)
    o_ref[...] = jnp.dot(s, w_ref[...], precision=lax.Precision.HIGHEST,
                         preferred_element_type=F32) + b_ref[...]


def _mod_vectors(cvec, ada_w, ada_b):
    depth, d, n = ada_w.shape
    rows = cvec.shape[0]
    tn = 1536
    return pl.pallas_call(
        _mod_kernel,
        out_shape=jax.ShapeDtypeStruct((depth, rows, n), F32),
        grid=(depth, n // tn),
        in_specs=[pl.BlockSpec((rows, d), lambda l, j: (0, 0)),
                  pl.BlockSpec((None, d, tn), lambda l, j: (l, 0, j)),
                  pl.BlockSpec((None, 1, tn), lambda l, j: (l, 0, j))],
        out_specs=pl.BlockSpec((None, rows, tn), lambda l, j: (l, 0, j)),
        compiler_params=_cparams(2),
        name="mod_vectors",
    )(cvec, ada_w, ada_b.reshape(depth, 1, n))


def _group_sumsq(a, e):
    sq = a * a
    hi = sq.astype(BF16)
    lo = (sq - hi.astype(F32)).astype(BF16)
    return (jnp.dot(hi, e, preferred_element_type=F32) + jnp.dot(lo, e, preferred_element_type=F32))


def _attn_proj_kernel(x_ref, sh_ref, sc_ref, nw_ref, w_ref, cos_ref, sin_ref,
                      gq_ref, gqs_ref, gk_ref, gks_ref, e_ref,
                      qd_ref, kd_ref, va_ref, qg_ref, kg_ref, vg_ref):
    x = x_ref[...]
    h = (_rms(x, nw_ref[...]) * (1.0 + sc_ref[...]) + sh_ref[...]).astype(BF16)
    a = jnp.dot(h, w_ref[...], preferred_element_type=F32)
    cos = cos_ref[...]
    sin = sin_ref[...]
    scale = QK_SCALE

    for j in range(4):
        r = (a[:, C_Q1 + QK_A * j:C_Q1 + QK_A * (j + 1)] * cos
             + a[:, R_QK + QK_A * j:R_QK + QK_A * (j + 1)] * sin)
        if j < 2:
            r = r * scale
        r = r.astype(BF16)
        dst = qd_ref if j < 2 else kd_ref
        for hh in range(N_DIFF):
            dst[(j % 2) * N_DIFF + hh] = r[:, HEAD * hh:HEAD * (hh + 1)]
    t = x.shape[0]
    ext = (lax.broadcasted_iota(jnp.int32, (V_EXT, t), 0) == 0).astype(BF16)
    for hh in range(N_DIFF):
        va_ref[hh, 0, :DIFF_V, :] = a[:, C_VA + DIFF_V * hh:C_VA + DIFF_V * (hh + 1)].T.astype(BF16)
        va_ref[hh, 0, DIFF_V:, :] = ext

    e = e_ref[...]
    cos2 = jnp.concatenate([cos, cos], axis=1)
    sin2 = jnp.concatenate([sin, sin], axis=1)
    qb = a[:, C_QB:C_KB]
    rq = lax.rsqrt(_group_sumsq(qb, e) * (1.0 / HEAD) + EPS)
    q = (rq * (qb * (gq_ref[...] * cos2) + a[:, R_QB:R_KB] * (gqs_ref[...] * sin2)) * scale).astype(BF16)
    for hh in range(N_GQ):
        qg_ref[hh] = q[:, HEAD * hh:HEAD * (hh + 1)]
    nk = N_GKV * HEAD
    kb = a[:, C_KB:C_VB]
    rk = lax.rsqrt(_group_sumsq(kb, e[:nk, :nk]) * (1.0 / HEAD) + EPS)
    k = (rk * (kb * (gk_ref[...] * cos[:, :nk]) + a[:, R_KB:R_END] * (gks_ref[...] * sin[:, :nk]))).astype(BF16)
    vbt = a[:, C_VB:C_END].T.astype(BF16)
    for hh in range(N_GKV):
        kg_ref[hh] = k[:, HEAD * hh:HEAD * (hh + 1)]
        vg_ref[hh, 0, :HEAD, :] = vbt[HEAD * hh:HEAD * (hh + 1), :]
        vg_ref[hh, 0, HEAD:, :] = ext


def _attn_proj(x, sh, sc, nw, w_cat, cos, sin, gq, gqs, gk, gks, e, tile):
    b, s, d = x.shape
    t = min(tile, s)
    hd = lambda n, w: jax.ShapeDtypeStruct((b, n, s, w), BF16)
    hspec = lambda n, w: pl.BlockSpec((None, n, t, w), lambda bi, i: (bi, 0, i, 0))
    vd = lambda n, w: jax.ShapeDtypeStruct((b, n, s // t, w + V_EXT, t), BF16)
    vspec = lambda n, w: pl.BlockSpec((None, n, 1, w + V_EXT, t), lambda bi, i: (bi, 0, i, 0, 0))
    vec = pl.BlockSpec((None, 1, d), lambda bi, i: (bi, 0, 0))
    return pl.pallas_call(
        _attn_proj_kernel,
        out_shape=(hd(2 * N_DIFF, HEAD), hd(2 * N_DIFF, HEAD), vd(N_DIFF, DIFF_V),
                   hd(N_GQ, HEAD), hd(N_GKV, HEAD), vd(N_GKV, HEAD)),
        grid=(b, s // t),
        in_specs=[pl.BlockSpec((None, t, d), lambda bi, i: (bi, i, 0)), vec, vec,
                  _const_spec((1, d)), _const_spec(w_cat.shape),
                  pl.BlockSpec((t, 4 * HEAD), lambda bi, i: (i, 0)),
                  pl.BlockSpec((t, 4 * HEAD), lambda bi, i: (i, 0)),
                  _const_spec(gq.shape), _const_spec(gqs.shape),
                  _const_spec(gk.shape), _const_spec(gks.shape), _const_spec(e.shape)],
        out_specs=(hspec(2 * N_DIFF, HEAD), hspec(2 * N_DIFF, HEAD), vspec(N_DIFF, DIFF_V),
                   hspec(N_GQ, HEAD), hspec(N_GKV, HEAD), vspec(N_GKV, HEAD)),
        compiler_params=_cparams(2),
        name="attn_proj",
    )(x, sh, sc, nw, w_cat, cos, sin, gq, gqs, gk, gks, e)


def _flash_kernel(*refs, group, tq, dv, n_lat, unroll):
    if n_lat:
        q_ref, kc_ref, vc_ref, kl_ref, vl_ref, o_ref, m_scr, acc_scr, st0, st1, mx0, mx1 = refs
    else:
        q_ref, kc_ref, vc_ref, o_ref, m_scr, acc_scr = refs
    cols = group * tq
    q = q_ref[...].reshape(cols, HEAD)
    m_scr[...] = jnp.full(m_scr.shape, -jnp.inf, F32)
    acc_scr[...] = jnp.zeros(acc_scr.shape, F32)

    def scores(k):
        st = lax.dot_general(k, q, (((1,), (1,)), ((), ())), preferred_element_type=F32)
        return st, jnp.max(st, axis=0, keepdims=True)

    def consume(st, mx, vt):
        m_old = m_scr[...]
        m_new = jnp.maximum(m_old, mx)
        alpha = jnp.exp2(m_old - m_new)
        pt = jnp.exp2(st - m_new).astype(BF16)
        acc_scr[...] = alpha * acc_scr[...] + jnp.dot(vt, pt, preferred_element_type=F32)
        m_scr[...] = m_new

    consume(*scores(kc_ref[...]), vc_ref[...])
    if n_lat:
        tk = vl_ref.shape[2]

        def produce(j, st_ref, mx_ref):
            start = pl.multiple_of(j * tk, tk)
            st_ref[...], mx_ref[...] = scores(kl_ref[pl.ds(start, tk), :])

        bufs = ((st0, mx0), (st1, mx1))

        def run(j, produce_next):
            for u in range(unroll):
                if u < unroll - 1 or produce_next:
                    produce(j + u + 1, *bufs[(u + 1) % 2])
                st_ref, mx_ref = bufs[u % 2]
                consume(st_ref[...], mx_ref[...], vl_ref[j + u])

        produce(0, st0, mx0)

        def body(jj, carry):
            run(unroll * jj, True)
            return carry

        lax.fori_loop(0, n_lat // unroll - 1, body, 0)
        run(n_lat - unroll, False)
    acc = acc_scr[...]
    o = (acc[:dv] / acc[dv:dv + 1]).T
    if group > 1:
        o = jnp.concatenate([o[g * tq:(g + 1) * tq] for g in range(group)], axis=1)
    o_ref[...] = o.astype(o_ref.dtype)


def _flash(q, kc, vc, kl, vl, group, tq, out_dtype):
    b, hq, s, _ = q.shape
    hk, sc = kc.shape[1], kc.shape[2]
    hv, dvx = vc.shape[1], vc.shape[3]
    dv = dvx - V_EXT
    tq = min(tq, s)
    cols = group * tq
    n_lat = 0 if kl is None else vl.shape[2]
    in_specs = [pl.BlockSpec((None, group, tq, HEAD), lambda bi, h, i: (bi, h, i, 0)),
                pl.BlockSpec((None, None, sc, HEAD), lambda bi, h, i: (bi, h, 0, 0)),
                pl.BlockSpec((None, None, None, dvx, sc), lambda bi, h, i: (bi, h % hv, 0, 0, 0))]
    args = [q, kc, vc]
    scratch = [pltpu.VMEM((1, cols), F32), pltpu.VMEM((dvx, cols), F32)]
    if n_lat:
        assert n_lat % 2 == 0, "latent key chunks alternate between two score buffers"
        sl, tk = kl.shape[2], vl.shape[4]
        in_specs += [pl.BlockSpec((None, None, sl, HEAD), lambda bi, h, i: (bi, h, 0, 0)),
                     pl.BlockSpec((None, None, n_lat, dvx, tk), lambda bi, h, i: (bi, h % hv, 0, 0, 0))]
        args += [kl, vl]
        scratch += [pltpu.VMEM((tk, cols), F32), pltpu.VMEM((tk, cols), F32),
                    pltpu.VMEM((1, cols), F32), pltpu.VMEM((1, cols), F32)]
    unroll = 4 if n_lat % 4 == 0 else 2
    kern = functools.partial(_flash_kernel, group=group, tq=tq, dv=dv, n_lat=n_lat, unroll=unroll)
    return pl.pallas_call(
        kern,
        out_shape=jax.ShapeDtypeStruct((b, s, hq * dv), out_dtype),
        grid=(b, hk, s // tq),
        in_specs=in_specs,
        out_specs=pl.BlockSpec((None, tq, group * dv), lambda bi, h, i: (bi, i, h)),
        scratch_shapes=scratch,
        compiler_params=_cparams(3),
        name="flash_g%d_l%d" % (group, n_lat),
    )(*args)


def _attn_out_kernel(od_ref, og_ref, lq1_ref, lk1_ref, lq2_ref, lk2_ref, subln_ref, w_ref,
                     x_ref, g1_ref, o_ref, *, lam_init):
    lam = (jnp.exp(jnp.sum(lq1_ref[...] * lk1_ref[...], axis=-1, keepdims=True))
           - jnp.exp(jnp.sum(lq2_ref[...] * lk2_ref[...], axis=-1, keepdims=True)) + lam_init)
    od = od_ref[...]
    half = N_DIFF * DIFF_V
    parts = []
    for hh in range(N_DIFF):
        dh = od[:, DIFF_V * hh:DIFF_V * (hh + 1)] - lam * od[:, half + DIFF_V * hh:half + DIFF_V * (hh + 1)]
        parts.append(_rms(dh, subln_ref[...]) * (1.0 - lam_init))
    att = jnp.concatenate([p.astype(BF16) for p in parts] + [og_ref[...]], axis=1)
    y = jnp.dot(att, w_ref[...], preferred_element_type=F32)
    o_ref[...] = x_ref[...] + g1_ref[...] * y


def _attn_out(od, og, lq1, lk1, lq2, lk2, subln, w_out, x, g1, lam_init, tile):
    b, s, d = x.shape
    t = min(tile, s)
    vec = pl.BlockSpec((None, 1, d), lambda bi, i: (bi, 0, 0))
    row = lambda a: pl.BlockSpec((None, t, a.shape[2]), lambda bi, i: (bi, i, 0))
    return pl.pallas_call(
        functools.partial(_attn_out_kernel, lam_init=lam_init),
        out_shape=jax.ShapeDtypeStruct(x.shape, F32),
        grid=(b, s // t),
        in_specs=[row(od), row(og), _const_spec(lq1.shape), _const_spec(lk1.shape),
                  _const_spec(lq2.shape), _const_spec(lk2.shape), _const_spec(subln.shape),
                  _const_spec(w_out.shape), row(x), vec],
        out_specs=row(x),
        compiler_params=_cparams(2),
        name="attn_out",
    )(od, og, lq1, lk1, lq2, lk2, subln, w_out, x, g1)


def _sgu_kernel(x_ref, sh_ref, sc_ref, g1_ref, nw_ref, win_ref, vn_ref, ws_ref, bs_ref, wout_ref, o_ref):
    x = x_ref[...]
    t, d = x.shape
    h = (_rms(x, nw_ref[...]) * (1.0 + sc_ref[...]) + sh_ref[...]).astype(BF16)
    z = jnp.dot(h, win_ref[...], preferred_element_type=F32)
    z = 0.5 * z * (1.0 + lax.erf(z * (2.0 ** -0.5)))
    sd = z.shape[1] // 2
    u = z[:, :sd]
    v = _rms(z[:, sd:], vn_ref[...]).astype(BF16)
    gw = sd // SGU_GROUPS
    rows = []
    for n in range(t // SGU_CHUNK):
        cols = []
        for g in range(SGU_GROUPS):
            vg = v[SGU_CHUNK * n:SGU_CHUNK * (n + 1), gw * g:gw * (g + 1)]
            cols.append(jnp.dot(ws_ref[g], vg, preferred_element_type=F32))
        rows.append(jnp.concatenate(cols, axis=1) + bs_ref[...])
    mixed = jnp.concatenate(rows, axis=0) if len(rows) > 1 else rows[0]
    y = jnp.dot((u * mixed).astype(BF16), wout_ref[...], preferred_element_type=F32)
    o_ref[...] = x + g1_ref[...] * y


def _sgu(x, sh, sc, g1, nw, w_in, vn, w_s, bs_full, w_out, tile):
    b, s, d = x.shape
    t = min(tile, s)
    vec = pl.BlockSpec((None, 1, d), lambda bi, i: (bi, 0, 0))
    row = pl.BlockSpec((None, t, d), lambda bi, i: (bi, i, 0))
    return pl.pallas_call(
        _sgu_kernel,
        out_shape=jax.ShapeDtypeStruct(x.shape, F32),
        grid=(b, s // t),
        in_specs=[row, vec, vec, vec, _const_spec(nw.shape), _const_spec(w_in.shape),
                  _const_spec(vn.shape), _const_spec(w_s.shape), _const_spec(bs_full.shape),
                  _const_spec(w_out.shape)],
        out_specs=row,
        compiler_params=_cparams(2),
        name="sgu",
    )(x, sh, sc, g1, nw, w_in, vn, w_s, bs_full, w_out)


def _ffn_kernel(xp_ref, x_ref, xn_ref, sh_ref, sc_ref, g2_ref, nw_ref, wup_ref, cw_ref, cb_ref,
                wdn_ref, fn_ref, o_ref, z_scr, *, final_norm):
    i = pl.program_id(1)
    last = pl.num_programs(1) - 1
    x = x_ref[...]
    t = x.shape[0]
    xa = jnp.concatenate([xp_ref[...], x, xn_ref[...]], axis=0)
    h = _rms(xa, nw_ref[...]) * (1.0 + sc_ref[...]) + sh_ref[...]
    r = lax.broadcasted_iota(jnp.int32, (t + 2 * CONV_HALO, 1), 0)
    outside = ((r < CONV_HALO) & (i == 0)) | ((r >= t + CONV_HALO) & (i == last))
    h = jnp.where(outside, 0.0, h).astype(BF16)
    z_scr[...] = jnp.dot(h, wup_ref[...], preferred_element_type=F32)
    cw = cw_ref[...]
    zc = (cw[0:1] * z_scr[pl.ds(CONV_HALO - 1, t), :] + cw[1:2] * z_scr[pl.ds(CONV_HALO, t), :]
          + cw[2:3] * z_scr[pl.ds(CONV_HALO + 1, t), :] + cb_ref[...])
    f = zc.shape[1] // 2
    g = zc[:, :f]
    act = (g * _sigmoid(g) * zc[:, f:]).astype(BF16)
    y = jnp.dot(act, wdn_ref[...], preferred_element_type=F32)
    out = x + g2_ref[...] * y
    if final_norm:
        out = _rms(out, fn_ref[...])
    o_ref[...] = out


def _ffn(x, sh, sc, g2, nw, w_up, conv_w, conv_b, w_dn, fn, final_norm, tile):
    b, s, d = x.shape
    t = min(tile, s)
    tb = t // CONV_HALO
    nhb = s // CONV_HALO
    vec = pl.BlockSpec((None, 1, d), lambda bi, i: (bi, 0, 0))
    row = pl.BlockSpec((None, t, d), lambda bi, i: (bi, i, 0))
    prev = pl.BlockSpec((None, CONV_HALO, d), lambda bi, i: (bi, jnp.maximum(i * tb - 1, 0), 0))
    nxt = pl.BlockSpec((None, CONV_HALO, d), lambda bi, i: (bi, jnp.minimum((i + 1) * tb, nhb - 1), 0))
    return pl.pallas_call(
        functools.partial(_ffn_kernel, final_norm=final_norm),
        out_shape=jax.ShapeDtypeStruct(x.shape, F32),
        grid=(b, s // t),
        in_specs=[prev, row, nxt, vec, vec, vec, _const_spec(nw.shape), _const_spec(w_up.shape),
                  _const_spec(conv_w.shape), _const_spec(conv_b.shape), _const_spec(w_dn.shape),
                  _const_spec(fn.shape)],
        out_specs=row,
        scratch_shapes=[pltpu.VMEM((t + 2 * CONV_HALO, w_up.shape[1]), F32)],
        compiler_params=_cparams(2),
        name="ffn",
    )(x, x, x, sh, sc, g2, nw, w_up, conv_w, conv_b, w_dn, fn)


def _pair_swap_cols(w):
    d, n = w.shape
    w2 = w.reshape(d, n // 2, 2)
    return jnp.stack([-w2[..., 1], w2[..., 0]], axis=-1).reshape(d, n)


def _pair_swap_vec(g):
    return g.reshape(-1, 2)[:, ::-1].reshape(-1)


def _rope_tables(rows):
    n_freq = HEAD // 4
    inv = ROPE_THETA ** (-jnp.arange(n_freq, dtype=F32) / n_freq)
    row = jnp.repeat(jnp.arange(rows, dtype=F32), GRID_COLS)
    col = jnp.tile(jnp.arange(GRID_COLS, dtype=F32), rows)
    ang = jnp.concatenate([row[:, None] * inv, col[:, None] * inv], axis=-1)
    expand = lambda t: jnp.tile(jnp.repeat(t, 2, axis=-1), (1, N_DIFF))
    return expand(jnp.cos(ang)), expand(jnp.sin(ang))


def kernel(x, c, ctx, c_ctx, ada_w, ada_b, mix_norm, ffn_norm, final_norm, attn_w_in, attn_w_out,
           diff_lq1, diff_lk1, diff_lq2, diff_lk2, diff_subln, gqa_q_norm, gqa_k_norm, sgu_w_in,
           sgu_v_norm, sgu_w_s, sgu_b_s, sgu_w_out, ffn_w_up, ffn_conv_w, ffn_conv_b, ffn_w_down):
    b, s, d = x.shape
    n_ctx = ctx.shape[1]
    depth = ada_w.shape[0]
    last_attn = (depth - 1) // 2 * 2

    rows_pad = -(-(b + 1) // 8) * 8
    cvec = jnp.zeros((rows_pad, d), F32).at[:b].set(c).at[b].set(c_ctx)
    mods = _mod_vectors(cvec, ada_w, ada_b)

    cos_lat, sin_lat = _rope_tables(s // GRID_COLS)
    cos_ctx = jnp.ones((n_ctx, 4 * HEAD), F32)
    sin_ctx = jnp.zeros((n_ctx, 4 * HEAD), F32)
    eye = jnp.repeat(jnp.repeat(jnp.eye(N_GQ, dtype=BF16), HEAD, axis=0), HEAD, axis=1)
    fn = final_norm.reshape(1, d)

    def split_mod(m):
        return [m[:, k * d:(k + 1) * d].reshape(b, 1, d) for k in range(N_MOD)]

    for l in range(depth):
        i = l // 2
        is_attn = l % 2 == 0
        update_ctx = l < last_attn
        sh1, sc1, g1, sh2, sc2, g2 = split_mod(mods[l, :b])
        need_ctx = is_attn or update_ctx
        if need_ctx:
            csh1, csc1, cg1, csh2, csc2, cg2 = split_mod(jnp.broadcast_to(mods[l, b], (b, N_MOD * d)))
        nw1 = mix_norm[l].reshape(1, d)
        nw2 = ffn_norm[l].reshape(1, d)

        if is_attn:
            lam_init = 0.8 - 0.6 * math.exp(-0.3 * l)
            w = attn_w_in[i]
            w_cat = jnp.concatenate(
                [w, _pair_swap_cols(w[:, C_Q1:C_VA]), _pair_swap_cols(w[:, C_QB:C_KB]),
                 _pair_swap_cols(w[:, C_KB:C_VB])], axis=1).astype(BF16)
            gq = jnp.tile(gqa_q_norm[i], N_GQ).reshape(1, -1)
            gqs = jnp.tile(_pair_swap_vec(gqa_q_norm[i]), N_GQ).reshape(1, -1)
            gk = jnp.tile(gqa_k_norm[i], N_GKV).reshape(1, -1)
            gks = jnp.tile(_pair_swap_vec(gqa_k_norm[i]), N_GKV).reshape(1, -1)
            w_out = attn_w_out[i].astype(BF16)
            vecs = [v[i].reshape(1, -1) for v in (diff_lq1, diff_lk1, diff_lq2, diff_lk2, diff_subln)]

            lat = _attn_proj(x, sh1, sc1, nw1, w_cat, cos_lat, sin_lat, gq, gqs, gk, gks, eye, KV_TILE)
            cpr = _attn_proj(ctx, csh1, csc1, nw1, w_cat, cos_ctx, sin_ctx, gq, gqs, gk, gks, eye, n_ctx)
            od = _flash(lat[0], cpr[1], cpr[2], lat[1], lat[2], 1, 1024, F32)
            og = _flash(lat[3], cpr[4], cpr[5], lat[4], lat[5], GQ_GROUP, 256, BF16)
            x_new = _attn_out(od, og, *vecs, w_out, x, g1, lam_init, 512)
            if update_ctx:
                cod = _flash(cpr[0], cpr[1], cpr[2], None, None, 1, 1024, F32)
                cog = _flash(cpr[3], cpr[4], cpr[5], None, None, GQ_GROUP, 256, BF16)
                ctx = _attn_out(cod, cog, *vecs, w_out, ctx, cg1, lam_init, 256)
            x = x_new
        else:
            w_in = sgu_w_in[i].astype(BF16)
            w_s = sgu_w_s[i].astype(BF16)
            w_out = sgu_w_out[i].astype(BF16)
            vn = sgu_v_norm[i].reshape(1, -1)
            gw = sgu_w_out.shape[1] // SGU_GROUPS
            bs_full = jnp.repeat(sgu_b_s[i].T, gw, axis=1)
            x_new = _sgu(x, sh1, sc1, g1, nw1, w_in, vn, w_s, bs_full, w_out, 256)
            if update_ctx:
                ctx = _sgu(ctx, csh1, csc1, cg1, nw1, w_in, vn, w_s, bs_full, w_out, 256)
            x = x_new

        w_up = ffn_w_up[l].astype(BF16)
        w_dn = ffn_w_down[l].astype(BF16)
        cb = ffn_conv_b[l].reshape(1, -1)
        x = _ffn(x, sh2, sc2, g2, nw2, w_up, ffn_conv_w[l], cb, w_dn, fn, l == depth - 1, 256)
        if update_ctx:
            ctx = _ffn(ctx, csh2, csc2, cg2, nw2, w_up, ffn_conv_w[l], cb, w_dn, fn, False, 256)
    return x
```

```python
import functools
import math

import jax
import jax.numpy as jnp
from jax import lax
from jax.experimental import pallas as pl
from jax.experimental.pallas import tpu as pltpu

F32 = jnp.float32
BF16 = jnp.bfloat16

EPS = 1e-6
HEAD = 64
N_DIFF = 4
DIFF_V = 2 * HEAD
N_GQ = 8
N_GKV = 2
GQ_GROUP = N_GQ // N_GKV
ROPE_THETA = 10000.0
GRID_COLS = 64
N_MOD = 6
SGU_CHUNK = 128
SGU_GROUPS = 4
CONV_HALO = 8
FFN_CHUNKS = 1
V_EXT = 16
KV_TILE = 512
QK_SCALE = HEAD ** -0.5 * math.log2(math.e)
VMEM_LIMIT = 56 * 1024 * 1024

QK_A = N_DIFF * HEAD
C_Q1, C_Q2, C_K1, C_K2 = 0, QK_A, 2 * QK_A, 3 * QK_A
C_VA = 4 * QK_A
C_QB = C_VA + N_DIFF * DIFF_V
C_KB = C_QB + N_GQ * HEAD
C_VB = C_KB + N_GKV * HEAD
C_END = C_VB + N_GKV * HEAD
R_QK = C_END
R_QB = R_QK + 4 * QK_A
R_KB = R_QB + N_GQ * HEAD
R_END = R_KB + N_GKV * HEAD


def _cparams(n_axes):
    return pltpu.CompilerParams(dimension_semantics=("arbitrary",) * n_axes,
                                vmem_limit_bytes=VMEM_LIMIT)


def _const_spec(shape):
    nd = len(shape)
    return pl.BlockSpec(shape, lambda *_: (0,) * nd, pipeline_mode=pl.Buffered(1))


def _rms(xf, g):
    ms = jnp.mean(xf * xf, axis=-1, keepdims=True)
    return xf * lax.rsqrt(ms + EPS) * g


def _sigmoid(x):
    return 1.0 / (1.0 + jnp.exp(-x))


def _mod_kernel(c_ref, w_ref, b_ref, o_ref):
    c = c_ref[...]
    s = c * _sigmoid(c)
    o_ref[...] = jnp.dot(s, w_ref[...], precision=lax.Precision.HIGHEST,
                         preferred_element_type=F32) + b_ref[...]


def _mod_vectors(cvec, ada_w, ada_b):
    depth, d, n = ada_w.shape
    rows = cvec.shape[0]
    tn = 1536
    return pl.pallas_call(
        _mod_kernel,
        out_shape=jax.ShapeDtypeStruct((depth, rows, n), F32),
        grid=(depth, n // tn),
        in_specs=[pl.BlockSpec((rows, d), lambda l, j: (0, 0)),
                  pl.BlockSpec((None, d, tn), lambda l, j: (l, 0, j)),
                  pl.BlockSpec((None, 1, tn), lambda l, j: (l, 0, j))],
        out_specs=pl.BlockSpec((None, rows, tn), lambda l, j: (l, 0, j)),
        compiler_params=_cparams(2),
        name="mod_vectors",
    )(cvec, ada_w, ada_b.reshape(depth, 1, n))


def _group_sumsq(a, e):
    sq = a * a
    hi = sq.astype(BF16)
    lo = (sq - hi.astype(F32)).astype(BF16)
    return (jnp.dot(hi, e, preferred_element_type=F32) + jnp.dot(lo, e, preferred_element_type=F32))


def _attn_proj_kernel(x_ref, sh_ref, sc_ref, nw_ref, w_ref, cos_ref, sin_ref,
                      gq_ref, gqs_ref, gk_ref, gks_ref, e_ref,
                      qd_ref, kd_ref, va_ref, qg_ref, kg_ref, vg_ref):
    x = x_ref[...]
    h = (_rms(x, nw_ref[...]) * (1.0 + sc_ref[...]) + sh_ref[...]).astype(BF16)
    a = jnp.dot(h, w_ref[...], preferred_element_type=F32)
    cos = cos_ref[...]
    sin = sin_ref[...]
    scale = QK_SCALE

    for j in range(4):
        r = (a[:, C_Q1 + QK_A * j:C_Q1 + QK_A * (j + 1)] * cos
             + a[:, R_QK + QK_A * j:R_QK + QK_A * (j + 1)] * sin)
        if j < 2:
            rt = (r * scale).T.astype(BF16)
            for hh in range(N_DIFF):
                qd_ref[j * N_DIFF + hh] = rt[HEAD * hh:HEAD * (hh + 1), :]
        else:
            r = r.astype(BF16)
            for hh in range(N_DIFF):
                kd_ref[(j - 2) * N_DIFF + hh] = r[:, HEAD * hh:HEAD * (hh + 1)]
    t = x.shape[0]
    ext = (lax.broadcasted_iota(jnp.int32, (V_EXT, t), 0) == 0).astype(BF16)
    for hh in range(N_DIFF):
        va_ref[hh, 0, :DIFF_V, :] = a[:, C_VA + DIFF_V * hh:C_VA + DIFF_V * (hh + 1)].T.astype(BF16)
        va_ref[hh, 0, DIFF_V:, :] = ext

    e = e_ref[...]
    cos2 = jnp.concatenate([cos, cos], axis=1)
    sin2 = jnp.concatenate([sin, sin], axis=1)
    qb = a[:, C_QB:C_KB]
    rq = lax.rsqrt(_group_sumsq(qb, e) * (1.0 / HEAD) + EPS)
    qt = (rq * (qb * (gq_ref[...] * cos2) + a[:, R_QB:R_KB] * (gqs_ref[...] * sin2)) * scale).T.astype(BF16)
    for hh in range(N_GQ):
        qg_ref[hh] = qt[HEAD * hh:HEAD * (hh + 1), :]
    nk = N_GKV * HEAD
    kb = a[:, C_KB:C_VB]
    rk = lax.rsqrt(_group_sumsq(kb, e[:nk, :nk]) * (1.0 / HEAD) + EPS)
    k = (rk * (kb * (gk_ref[...] * cos[:, :nk]) + a[:, R_KB:R_END] * (gks_ref[...] * sin[:, :nk]))).astype(BF16)
    vbt = a[:, C_VB:C_END].T.astype(BF16)
    for hh in range(N_GKV):
        kg_ref[hh] = k[:, HEAD * hh:HEAD * (hh + 1)]
        vg_ref[hh, 0, :HEAD, :] = vbt[HEAD * hh:HEAD * (hh + 1), :]
        vg_ref[hh, 0, HEAD:, :] = ext


def _attn_proj(x, sh, sc, nw, w_cat, cos, sin, gq, gqs, gk, gks, e, tile):
    b, s, d = x.shape
    t = min(tile, s)
    hd = lambda n, w: jax.ShapeDtypeStruct((b, n, s, w), BF16)
    hspec = lambda n, w: pl.BlockSpec((None, n, t, w), lambda bi, i: (bi, 0, i, 0))
    qd = lambda n: jax.ShapeDtypeStruct((b, n, HEAD, s), BF16)
    qspec = lambda n: pl.BlockSpec((None, n, HEAD, t), lambda bi, i: (bi, 0, 0, i))
    vd = lambda n, w: jax.ShapeDtypeStruct((b, n, s // t, w + V_EXT, t), BF16)
    vspec = lambda n, w: pl.BlockSpec((None, n, 1, w + V_EXT, t), lambda bi, i: (bi, 0, i, 0, 0))
    vec = pl.BlockSpec((None, 1, d), lambda bi, i: (bi, 0, 0))
    return pl.pallas_call(
        _attn_proj_kernel,
        out_shape=(qd(2 * N_DIFF), hd(2 * N_DIFF, HEAD), vd(N_DIFF, DIFF_V),
                   qd(N_GQ), hd(N_GKV, HEAD), vd(N_GKV, HEAD)),
        grid=(b, s // t),
        in_specs=[pl.BlockSpec((None, t, d), lambda bi, i: (bi, i, 0)), vec, vec,
                  _const_spec((1, d)), _const_spec(w_cat.shape),
                  pl.BlockSpec((t, 4 * HEAD), lambda bi, i: (i, 0)),
                  pl.BlockSpec((t, 4 * HEAD), lambda bi, i: (i, 0)),
                  _const_spec(gq.shape), _const_spec(gqs.shape),
                  _const_spec(gk.shape), _const_spec(gks.shape), _const_spec(e.shape)],
        out_specs=(qspec(2 * N_DIFF), hspec(2 * N_DIFF, HEAD), vspec(N_DIFF, DIFF_V),
                   qspec(N_GQ), hspec(N_GKV, HEAD), vspec(N_GKV, HEAD)),
        compiler_params=_cparams(2),
        name="attn_proj",
    )(x, sh, sc, nw, w_cat, cos, sin, gq, gqs, gk, gks, e)


def _flash_kernel(*refs, group, tq, dv, n_lat, unroll):
    if n_lat:
        q_ref, kc_ref, vc_ref, kl_ref, vl_ref, o_ref, m_scr, acc_scr, st0, st1, mx0, mx1 = refs
    else:
        q_ref, kc_ref, vc_ref, o_ref, m_scr, acc_scr = refs
    cols = group * tq
    qt = q_ref[0] if group == 1 else jnp.concatenate([q_ref[g] for g in range(group)], axis=1)
    m_scr[...] = jnp.full(m_scr.shape, -jnp.inf, F32)
    acc_scr[...] = jnp.zeros(acc_scr.shape, F32)

    def scores(k):
        st = jnp.dot(k, qt, preferred_element_type=F32)
        return st, jnp.max(st, axis=0, keepdims=True)

    def consume(st, mx, vt):
        m_old = m_scr[...]
        m_new = jnp.maximum(m_old, mx)
        alpha = jnp.exp2(m_old - m_new)
        pt = jnp.exp2(st - m_new).astype(BF16)
        acc_scr[...] = alpha * acc_scr[...] + jnp.dot(vt, pt, preferred_element_type=F32)
        m_scr[...] = m_new

    consume(*scores(kc_ref[...]), vc_ref[...])
    if n_lat:
        tk = vl_ref.shape[2]

        def produce(j, st_ref, mx_ref):
            start = pl.multiple_of(j * tk, tk)
            st_ref[...], mx_ref[...] = scores(kl_ref[pl.ds(start, tk), :])

        bufs = ((st0, mx0), (st1, mx1))

        def run(j, produce_next):
            for u in range(unroll):
                if u < unroll - 1 or produce_next:
                    produce(j + u + 1, *bufs[(u + 1) % 2])
                st_ref, mx_ref = bufs[u % 2]
                consume(st_ref[...], mx_ref[...], vl_ref[j + u])

        produce(0, st0, mx0)

        def body(jj, carry):
            run(unroll * jj, True)
            return carry

        lax.fori_loop(0, n_lat // unroll - 1, body, 0)
        run(n_lat - unroll, False)
    acc = acc_scr[...]
    o = (acc[:dv] / acc[dv:dv + 1]).T
    if group > 1:
        o = jnp.concatenate([o[g * tq:(g + 1) * tq] for g in range(group)], axis=1)
    o_ref[...] = o.astype(o_ref.dtype)


def _flash(q, kc, vc, kl, vl, group, tq, out_dtype):
    b, hq, _, s = q.shape
    hk, sc = kc.shape[1], kc.shape[2]
    hv, dvx = vc.shape[1], vc.shape[3]
    dv = dvx - V_EXT
    tq = min(tq, s)
    cols = group * tq
    n_lat = 0 if kl is None else vl.shape[2]
    in_specs = [pl.BlockSpec((None, group, HEAD, tq), lambda bi, h, i: (bi, h, 0, i)),
                pl.BlockSpec((None, None, sc, HEAD), lambda bi, h, i: (bi, h, 0, 0)),
                pl.BlockSpec((None, None, None, dvx, sc), lambda bi, h, i: (bi, h % hv, 0, 0, 0))]
    args = [q, kc, vc]
    scratch = [pltpu.VMEM((1, cols), F32), pltpu.VMEM((dvx, cols), F32)]
    if n_lat:
        assert n_lat % 2 == 0, "latent key chunks alternate between two score buffers"
        sl, tk = kl.shape[2], vl.shape[4]
        in_specs += [pl.BlockSpec((None, None, sl, HEAD), lambda bi, h, i: (bi, h, 0, 0)),
                     pl.BlockSpec((None, None, n_lat, dvx, tk), lambda bi, h, i: (bi, h % hv, 0, 0, 0))]
        args += [kl, vl]
        scratch += [pltpu.VMEM((tk, cols), F32), pltpu.VMEM((tk, cols), F32),
                    pltpu.VMEM((1, cols), F32), pltpu.VMEM((1, cols), F32)]
    unroll = next(u for u in (8, 4, 2) if n_lat % u == 0)
    kern = functools.partial(_flash_kernel, group=group, tq=tq, dv=dv, n_lat=n_lat, unroll=unroll)
    return pl.pallas_call(
        kern,
        out_shape=jax.ShapeDtypeStruct((b, s, hq * dv), out_dtype),
        grid=(b, hk, s // tq),
        in_specs=in_specs,
        out_specs=pl.BlockSpec((None, tq, group * dv), lambda bi, h, i: (bi, i, h)),
        scratch_shapes=scratch,
        compiler_params=_cparams(3),
        name="flash_g%d_l%d" % (group, n_lat),
    )(*args)


def _attn_out_kernel(od_ref, og_ref, lq1_ref, lk1_ref, lq2_ref, lk2_ref, subln_ref, w_ref,
                     x_ref, g1_ref, o_ref, *, lam_init):
    lam = (jnp.exp(jnp.sum(lq1_ref[...] * lk1_ref[...], axis=-1, keepdims=True))
           - jnp.exp(jnp.sum(lq2_ref[...] * lk2_ref[...], axis=-1, keepdims=True)) + lam_init)
    od = od_ref[...]
    half = N_DIFF * DIFF_V
    parts = []
    for hh in range(N_DIFF):
        dh = od[:, DIFF_V * hh:DIFF_V * (hh + 1)] - lam * od[:, half + DIFF_V * hh:half + DIFF_V * (hh + 1)]
        parts.append(_rms(dh, subln_ref[...]) * (1.0 - lam_init))
    att = jnp.concatenate([p.astype(BF16) for p in parts] + [og_ref[...]], axis=1)
    y = jnp.dot(att, w_ref[...], preferred_element_type=F32)
    o_ref[...] = x_ref[...] + g1_ref[...] * y


def _attn_out(od, og, lq1, lk1, lq2, lk2, subln, w_out, x, g1, lam_init, tile):
    b, s, d = x.shape
    t = min(tile, s)
    vec = pl.BlockSpec((None, 1, d), lambda bi, i: (bi, 0, 0))
    row = lambda a: pl.BlockSpec((None, t, a.shape[2]), lambda bi, i: (bi, i, 0))
    return pl.pallas_call(
        functools.partial(_attn_out_kernel, lam_init=lam_init),
        out_shape=jax.ShapeDtypeStruct(x.shape, F32),
        grid=(b, s // t),
        in_specs=[row(od), row(og), _const_spec(lq1.shape), _const_spec(lk1.shape),
                  _const_spec(lq2.shape), _const_spec(lk2.shape), _const_spec(subln.shape),
                  _const_spec(w_out.shape), row(x), vec],
        out_specs=row(x),
        compiler_params=_cparams(2),
        name="attn_out",
    )(od, og, lq1, lk1, lq2, lk2, subln, w_out, x, g1)


def _sgu_kernel(x_ref, sh_ref, sc_ref, g1_ref, nw_ref, win_ref, vn_ref, ws_ref, bs_ref, wout_ref, o_ref):
    x = x_ref[...]
    t, d = x.shape
    h = (_rms(x, nw_ref[...]) * (1.0 + sc_ref[...]) + sh_ref[...]).astype(BF16)
    z = jnp.dot(h, win_ref[...], preferred_element_type=F32)
    z = 0.5 * z * (1.0 + lax.erf(z * (2.0 ** -0.5)))
    sd = z.shape[1] // 2
    u = z[:, :sd]
    v = _rms(z[:, sd:], vn_ref[...]).astype(BF16)
    gw = sd // SGU_GROUPS
    rows = []
    for n in range(t // SGU_CHUNK):
        cols = []
        for g in range(SGU_GROUPS):
            vg = v[SGU_CHUNK * n:SGU_CHUNK * (n + 1), gw * g:gw * (g + 1)]
            cols.append(jnp.dot(ws_ref[g], vg, preferred_element_type=F32))
        rows.append(jnp.concatenate(cols, axis=1) + bs_ref[...])
    mixed = jnp.concatenate(rows, axis=0) if len(rows) > 1 else rows[0]
    y = jnp.dot((u * mixed).astype(BF16), wout_ref[...], preferred_element_type=F32)
    o_ref[...] = x + g1_ref[...] * y


def _sgu(x, sh, sc, g1, nw, w_in, vn, w_s, bs_full, w_out, tile):
    b, s, d = x.shape
    t = min(tile, s)
    vec = pl.BlockSpec((None, 1, d), lambda bi, i: (bi, 0, 0))
    row = pl.BlockSpec((None, t, d), lambda bi, i: (bi, i, 0))
    return pl.pallas_call(
        _sgu_kernel,
        out_shape=jax.ShapeDtypeStruct(x.shape, F32),
        grid=(b, s // t),
        in_specs=[row, vec, vec, vec, _const_spec(nw.shape), _const_spec(w_in.shape),
                  _const_spec(vn.shape), _const_spec(w_s.shape), _const_spec(bs_full.shape),
                  _const_spec(w_out.shape)],
        out_specs=row,
        compiler_params=_cparams(2),
        name="sgu",
    )(x, sh, sc, g1, nw, w_in, vn, w_s, bs_full, w_out)


def _ffn_kernel(xp_ref, x_ref, xn_ref, sh_ref, sc_ref, g2_ref, nw_ref, wup_ref, cw_ref, cb_ref,
                wdn_ref, fn_ref, o_ref, *z_scrs, final_norm):
    i = pl.program_id(1)
    last = pl.num_programs(1) - 1
    x = x_ref[...]
    t = x.shape[0]
    xa = jnp.concatenate([xp_ref[...], x, xn_ref[...]], axis=0)
    h = _rms(xa, nw_ref[...]) * (1.0 + sc_ref[...]) + sh_ref[...]
    r = lax.broadcasted_iota(jnp.int32, (t + 2 * CONV_HALO, 1), 0)
    outside = ((r < CONV_HALO) & (i == 0)) | ((r >= t + CONV_HALO) & (i == last))
    h = jnp.where(outside, 0.0, h).astype(BF16)
    fc = wdn_ref.shape[0] // len(z_scrs)
    y = None
    for c, z_scr in enumerate(z_scrs):
        cols = slice(2 * fc * c, 2 * fc * (c + 1))
        z_scr[...] = jnp.dot(h, wup_ref[:, cols], preferred_element_type=F32)
        cw = cw_ref[:, cols]
        zc = (cw[0:1] * z_scr[pl.ds(CONV_HALO - 1, t), :] + cw[1:2] * z_scr[pl.ds(CONV_HALO, t), :]
              + cw[2:3] * z_scr[pl.ds(CONV_HALO + 1, t), :] + cb_ref[:, cols])
        g = zc[:, :fc]
        act = (g * _sigmoid(g) * zc[:, fc:]).astype(BF16)
        yc = jnp.dot(act, wdn_ref[fc * c:fc * (c + 1), :], preferred_element_type=F32)
        y = yc if y is None else y + yc
    out = x + g2_ref[...] * y
    if final_norm:
        out = _rms(out, fn_ref[...])
    o_ref[...] = out


def _ffn(x, sh, sc, g2, nw, w_up, conv_w, conv_b, w_dn, fn, final_norm, tile):
    b, s, d = x.shape
    t = min(tile, s)
    tb = t // CONV_HALO
    nhb = s // CONV_HALO
    vec = pl.BlockSpec((None, 1, d), lambda bi, i: (bi, 0, 0))
    row = pl.BlockSpec((None, t, d), lambda bi, i: (bi, i, 0))
    prev = pl.BlockSpec((None, CONV_HALO, d), lambda bi, i: (bi, jnp.maximum(i * tb - 1, 0), 0))
    nxt = pl.BlockSpec((None, CONV_HALO, d), lambda bi, i: (bi, jnp.minimum((i + 1) * tb, nhb - 1), 0))
    return pl.pallas_call(
        functools.partial(_ffn_kernel, final_norm=final_norm),
        out_shape=jax.ShapeDtypeStruct(x.shape, F32),
        grid=(b, s // t),
        in_specs=[prev, row, nxt, vec, vec, vec, _const_spec(nw.shape), _const_spec(w_up.shape),
                  _const_spec(conv_w.shape), _const_spec(conv_b.shape), _const_spec(w_dn.shape),
                  _const_spec(fn.shape)],
        out_specs=row,
        scratch_shapes=[pltpu.VMEM((t + 2 * CONV_HALO, w_up.shape[1] // FFN_CHUNKS), F32)] * FFN_CHUNKS,
        compiler_params=_cparams(2),
        name="ffn",
    )(x, x, x, sh, sc, g2, nw, w_up, conv_w, conv_b, w_dn, fn)


def _pair_swap_cols(w):
    d, n = w.shape
    w2 = w.reshape(d, n // 2, 2)
    return jnp.stack([-w2[..., 1], w2[..., 0]], axis=-1).reshape(d, n)


def _group_gate_up(a):
    f = a.shape[-1] // 2
    fc = f // FFN_CHUNKS
    parts = []
    for c in range(FFN_CHUNKS):
        parts += [a[..., fc * c:fc * (c + 1)], a[..., f + fc * c:f + fc * (c + 1)]]
    return jnp.concatenate(parts, axis=-1)


def _pair_swap_vec(g):
    return g.reshape(-1, 2)[:, ::-1].reshape(-1)


def _rope_tables(rows):
    n_freq = HEAD // 4
    inv = ROPE_THETA ** (-jnp.arange(n_freq, dtype=F32) / n_freq)
    row = jnp.repeat(jnp.arange(rows, dtype=F32), GRID_COLS)
    col = jnp.tile(jnp.arange(GRID_COLS, dtype=F32), rows)
    ang = jnp.concatenate([row[:, None] * inv, col[:, None] * inv], axis=-1)
    expand = lambda t: jnp.tile(jnp.repeat(t, 2, axis=-1), (1, N_DIFF))
    return expand(jnp.cos(ang)), expand(jnp.sin(ang))


def kernel(x, c, ctx, c_ctx, ada_w, ada_b, mix_norm, ffn_norm, final_norm, attn_w_in, attn_w_out,
           diff_lq1, diff_lk1, diff_lq2, diff_lk2, diff_subln, gqa_q_norm, gqa_k_norm, sgu_w_in,
           sgu_v_norm, sgu_w_s, sgu_b_s, sgu_w_out, ffn_w_up, ffn_conv_w, ffn_conv_b, ffn_w_down):
    b, s, d = x.shape
    n_ctx = ctx.shape[1]
    depth = ada_w.shape[0]
    last_attn = (depth - 1) // 2 * 2

    rows_pad = -(-(b + 1) // 8) * 8
    cvec = jnp.zeros((rows_pad, d), F32).at[:b].set(c).at[b].set(c_ctx)
    mods = _mod_vectors(cvec, ada_w, ada_b)

    cos_lat, sin_lat = _rope_tables(s // GRID_COLS)
    cos_ctx = jnp.ones((n_ctx, 4 * HEAD), F32)
    sin_ctx = jnp.zeros((n_ctx, 4 * HEAD), F32)
    eye = jnp.repeat(jnp.repeat(jnp.eye(N_GQ, dtype=BF16), HEAD, axis=0), HEAD, axis=1)
    fn = final_norm.reshape(1, d)

    def split_mod(m):
        return [m[:, k * d:(k + 1) * d].reshape(b, 1, d) for k in range(N_MOD)]

    for l in range(depth):
        i = l // 2
        is_attn = l % 2 == 0
        update_ctx = l < last_attn
        sh1, sc1, g1, sh2, sc2, g2 = split_mod(mods[l, :b])
        need_ctx = is_attn or update_ctx
        if need_ctx:
            csh1, csc1, cg1, csh2, csc2, cg2 = split_mod(jnp.broadcast_to(mods[l, b], (b, N_MOD * d)))
        nw1 = mix_norm[l].reshape(1, d)
        nw2 = ffn_norm[l].reshape(1, d)

        if is_attn:
            lam_init = 0.8 - 0.6 * math.exp(-0.3 * l)
            w = attn_w_in[i]
            w_cat = jnp.concatenate(
                [w, _pair_swap_cols(w[:, C_Q1:C_VA]), _pair_swap_cols(w[:, C_QB:C_KB]),
                 _pair_swap_cols(w[:, C_KB:C_VB])], axis=1).astype(BF16)
            gq = jnp.tile(gqa_q_norm[i], N_GQ).reshape(1, -1)
            gqs = jnp.tile(_pair_swap_vec(gqa_q_norm[i]), N_GQ).reshape(1, -1)
            gk = jnp.tile(gqa_k_norm[i], N_GKV).reshape(1, -1)
            gks = jnp.tile(_pair_swap_vec(gqa_k_norm[i]), N_GKV).reshape(1, -1)
            w_out = attn_w_out[i].astype(BF16)
            vecs = [v[i].reshape(1, -1) for v in (diff_lq1, diff_lk1, diff_lq2, diff_lk2, diff_subln)]

            lat = _attn_proj(x, sh1, sc1, nw1, w_cat, cos_lat, sin_lat, gq, gqs, gk, gks, eye, KV_TILE)
            cpr = _attn_proj(ctx, csh1, csc1, nw1, w_cat, cos_ctx, sin_ctx, gq, gqs, gk, gks, eye, n_ctx)
            od = _flash(lat[0], cpr[1], cpr[2], lat[1], lat[2], 1, 1024, F32)
            og = _flash(lat[3], cpr[4], cpr[5], lat[4], lat[5], GQ_GROUP, 256, BF16)
            x_new = _attn_out(od, og, *vecs, w_out, x, g1, lam_init, 512)
            if update_ctx:
                cod = _flash(cpr[0], cpr[1], cpr[2], None, None, 1, 1024, F32)
                cog = _flash(cpr[3], cpr[4], cpr[5], None, None, GQ_GROUP, 256, BF16)
                ctx = _attn_out(cod, cog, *vecs, w_out, ctx, cg1, lam_init, 256)
            x = x_new
        else:
            w_in = sgu_w_in[i].astype(BF16)
            w_s = sgu_w_s[i].astype(BF16)
            w_out = sgu_w_out[i].astype(BF16)
            vn = sgu_v_norm[i].reshape(1, -1)
            gw = sgu_w_out.shape[1] // SGU_GROUPS
            bs_full = jnp.repeat(sgu_b_s[i].T, gw, axis=1)
            x_new = _sgu(x, sh1, sc1, g1, nw1, w_in, vn, w_s, bs_full, w_out, 256)
            if update_ctx:
                ctx = _sgu(ctx, csh1, csc1, cg1, nw1, w_in, vn, w_s, bs_full, w_out, 256)
            x = x_new

        w_up = _group_gate_up(ffn_w_up[l]).astype(BF16)
        w_dn = ffn_w_down[l].astype(BF16)
        cw = _group_gate_up(ffn_conv_w[l])
        cb = _group_gate_up(ffn_conv_b[l].reshape(1, -1))
        x = _ffn(x, sh2, sc2, g2, nw2, w_up, cw, cb, w_dn, fn, l == depth - 1, 512)
        if update_ctx:
            ctx = _ffn(ctx, csh2, csc2, cg2, nw2, w_up, cw, cb, w_dn, fn, False, 512)
    return x
```

```python
import functools
import math

import jax
import jax.numpy as jnp
from jax import lax
from jax.experimental import pallas as pl
from jax.experimental.pallas import tpu as pltpu

F32 = jnp.float32
BF16 = jnp.bfloat16

EPS = 1e-6
HEAD = 64
N_DIFF = 4
DIFF_V = 2 * HEAD
N_GQ = 8
N_GKV = 2
GQ_GROUP = N_GQ // N_GKV
ROPE_THETA = 10000.0
GRID_COLS = 64
N_MOD = 6
SGU_CHUNK = 128
SGU_GROUPS = 4
SUBLANES = 8
CONV_HALO = SUBLANES
V_EXT = 2 * SUBLANES
MOD_TILE_N = 1536
PROJ_TILE = 512
FLASH_COLS = 1024
ATTN_OUT_TILE = 1024
SGU_TILE = 512
FFN_TILE = 512
QK_SCALE = HEAD ** -0.5 * math.log2(math.e)
VMEM_LIMIT = 56 * 1024 * 1024

QK_A = N_DIFF * HEAD
C_Q1, C_Q2, C_K1, C_K2 = 0, QK_A, 2 * QK_A, 3 * QK_A
C_VA = 4 * QK_A
C_QB = C_VA + N_DIFF * DIFF_V
C_KB = C_QB + N_GQ * HEAD
C_VB = C_KB + N_GKV * HEAD
C_END = C_VB + N_GKV * HEAD


def _cparams(n_axes):
    return pltpu.CompilerParams(dimension_semantics=("arbitrary",) * n_axes,
                                vmem_limit_bytes=VMEM_LIMIT)


def _const_spec(shape):
    nd = len(shape)
    return pl.BlockSpec(shape, lambda *_: (0,) * nd, pipeline_mode=pl.Buffered(1))


def _rms(xf, g):
    ms = jnp.mean(xf * xf, axis=-1, keepdims=True)
    return xf * lax.rsqrt(ms + EPS) * g


def _sigmoid(x):
    return 1.0 / (1.0 + jnp.exp(-x))


def _mod_kernel(c_ref, w_ref, b_ref, o_ref):
    c = c_ref[...]
    s = c * _sigmoid(c)
    o_ref[...] = jnp.dot(s, w_ref[...], precision=lax.Precision.HIGHEST,
                         preferred_element_type=F32) + b_ref[...]


def _mod_vectors(cvec, ada_w, ada_b):
    depth, d, n = ada_w.shape
    rows = cvec.shape[0]
    tn = MOD_TILE_N
    return pl.pallas_call(
        _mod_kernel,
        out_shape=jax.ShapeDtypeStruct((depth, rows, n), F32),
        grid=(depth, n // tn),
        in_specs=[pl.BlockSpec((rows, d), lambda l, j: (0, 0)),
                  pl.BlockSpec((None, d, tn), lambda l, j: (l, 0, j)),
                  pl.BlockSpec((None, 1, tn), lambda l, j: (l, 0, j))],
        out_specs=pl.BlockSpec((None, rows, tn), lambda l, j: (l, 0, j)),
        compiler_params=_cparams(2),
        name="mod_vectors",
    )(cvec, ada_w, ada_b.reshape(depth, 1, n))


def _group_sumsq(a, e):
    sq = a * a
    hi = sq.astype(BF16)
    lo = (sq - hi.astype(F32)).astype(BF16)
    return (jnp.dot(hi, e, preferred_element_type=F32) + jnp.dot(lo, e, preferred_element_type=F32))


def _pair_partner(a):
    n = a.shape[1]
    even = lax.broadcasted_iota(jnp.int32, (1, n), 1) % 2 == 0
    return jnp.where(even, -pltpu.roll(a, n - 1, axis=1), pltpu.roll(a, 1, axis=1))


def _attn_proj_kernel(x_ref, sh_ref, sc_ref, nw_ref, w_ref, cos_ref, sin_ref,
                      gq_ref, gqs_ref, gk_ref, gks_ref, e_ref,
                      qd_ref, kd_ref, va_ref, qg_ref, kg_ref, vg_ref):
    x = x_ref[...]
    h = (_rms(x, nw_ref[...]) * (1.0 + sc_ref[...]) + sh_ref[...]).astype(BF16)
    a = jnp.dot(h, w_ref[...], preferred_element_type=F32)
    cos = cos_ref[...]
    sin = sin_ref[...]
    scale = QK_SCALE

    for j in range(4):
        aj = a[:, C_Q1 + QK_A * j:C_Q1 + QK_A * (j + 1)]
        r = aj * cos + _pair_partner(aj) * sin
        if j < 2:
            rt = (r * scale).T.astype(BF16)
            for hh in range(N_DIFF):
                qd_ref[j * N_DIFF + hh] = rt[HEAD * hh:HEAD * (hh + 1), :]
        else:
            r = r.astype(BF16)
            for hh in range(N_DIFF):
                kd_ref[(j - 2) * N_DIFF + hh] = r[:, HEAD * hh:HEAD * (hh + 1)]
    t = x.shape[0]
    ext = (lax.broadcasted_iota(jnp.int32, (V_EXT, t), 0) == 0).astype(BF16)
    for hh in range(N_DIFF):
        va_ref[hh, 0, :DIFF_V, :] = a[:, C_VA + DIFF_V * hh:C_VA + DIFF_V * (hh + 1)].T.astype(BF16)
        va_ref[hh, 0, DIFF_V:, :] = ext

    e = e_ref[...]
    cos2 = jnp.concatenate([cos, cos], axis=1)
    sin2 = jnp.concatenate([sin, sin], axis=1)
    qb = a[:, C_QB:C_KB]
    rq = lax.rsqrt(_group_sumsq(qb, e) * (1.0 / HEAD) + EPS)
    qt = (rq * (qb * (gq_ref[...] * cos2) + _pair_partner(qb) * (gqs_ref[...] * sin2)) * scale).T.astype(BF16)
    for hh in range(N_GQ):
        qg_ref[hh] = qt[HEAD * hh:HEAD * (hh + 1), :]
    nk = N_GKV * HEAD
    kb = a[:, C_KB:C_VB]
    rk = lax.rsqrt(_group_sumsq(kb, e[:nk, :nk]) * (1.0 / HEAD) + EPS)
    k = (rk * (kb * (gk_ref[...] * cos[:, :nk]) + _pair_partner(kb) * (gks_ref[...] * sin[:, :nk]))).astype(BF16)
    vbt = a[:, C_VB:C_END].T.astype(BF16)
    for hh in range(N_GKV):
        kg_ref[hh] = k[:, HEAD * hh:HEAD * (hh + 1)]
        vg_ref[hh, 0, :HEAD, :] = vbt[HEAD * hh:HEAD * (hh + 1), :]
        vg_ref[hh, 0, HEAD:, :] = ext


def _attn_proj(x, sh, sc, nw, w_qkv, cos, sin, gq, gqs, gk, gks, e, tile):
    b, s, d = x.shape
    t = min(tile, s)
    hd = lambda n, w: jax.ShapeDtypeStruct((b, n, s, w), BF16)
    hspec = lambda n, w: pl.BlockSpec((None, n, t, w), lambda bi, i: (bi, 0, i, 0))
    qd = lambda n: jax.ShapeDtypeStruct((b, n, HEAD, s), BF16)
    qspec = lambda n: pl.BlockSpec((None, n, HEAD, t), lambda bi, i: (bi, 0, 0, i))
    vd = lambda n, w: jax.ShapeDtypeStruct((b, n, s // t, w + V_EXT, t), BF16)
    vspec = lambda n, w: pl.BlockSpec((None, n, 1, w + V_EXT, t), lambda bi, i: (bi, 0, i, 0, 0))
    vec = pl.BlockSpec((None, 1, d), lambda bi, i: (bi, 0, 0))
    return pl.pallas_call(
        _attn_proj_kernel,
        out_shape=(qd(2 * N_DIFF), hd(2 * N_DIFF, HEAD), vd(N_DIFF, DIFF_V),
                   qd(N_GQ), hd(N_GKV, HEAD), vd(N_GKV, HEAD)),
        grid=(b, s // t),
        in_specs=[pl.BlockSpec((None, t, d), lambda bi, i: (bi, i, 0)), vec, vec,
                  _const_spec((1, d)), _const_spec(w_qkv.shape),
                  pl.BlockSpec((t, 4 * HEAD), lambda bi, i: (i, 0)),
                  pl.BlockSpec((t, 4 * HEAD), lambda bi, i: (i, 0)),
                  _const_spec(gq.shape), _const_spec(gqs.shape),
                  _const_spec(gk.shape), _const_spec(gks.shape), _const_spec(e.shape)],
        out_specs=(qspec(2 * N_DIFF), hspec(2 * N_DIFF, HEAD), vspec(N_DIFF, DIFF_V),
                   qspec(N_GQ), hspec(N_GKV, HEAD), vspec(N_GKV, HEAD)),
        compiler_params=_cparams(2),
        name="attn_proj",
    )(x, sh, sc, nw, w_qkv, cos, sin, gq, gqs, gk, gks, e)


def _flash_kernel(*refs, group, tq, dv, n_lat, unroll):
    if n_lat:
        q_ref, kc_ref, vc_ref, kl_ref, vl_ref, o_ref, m_scr, acc_scr, st0, st1, mx0, mx1 = refs
    else:
        q_ref, kc_ref, vc_ref, o_ref, m_scr, acc_scr = refs
    cols = group * tq
    qt = q_ref[0] if group == 1 else jnp.concatenate([q_ref[g] for g in range(group)], axis=1)
    m_scr[...] = jnp.full(m_scr.shape, -jnp.inf, F32)
    acc_scr[...] = jnp.zeros(acc_scr.shape, F32)

    def scores(k):
        st = jnp.dot(k, qt, preferred_element_type=F32)
        return st, jnp.max(st, axis=0, keepdims=True)

    def consume(st, mx, vt):
        m_old = m_scr[...]
        m_new = jnp.maximum(m_old, mx)
        alpha = jnp.exp2(m_old - m_new)
        pt = jnp.exp2(st - m_new).astype(BF16)
        acc_scr[...] = alpha * acc_scr[...] + jnp.dot(vt, pt, preferred_element_type=F32)
        m_scr[...] = m_new

    consume(*scores(kc_ref[...]), vc_ref[...])
    if n_lat:
        tk = vl_ref.shape[2]

        def produce(j, st_ref, mx_ref):
            start = pl.multiple_of(j * tk, tk)
            st_ref[...], mx_ref[...] = scores(kl_ref[pl.ds(start, tk), :])

        bufs = ((st0, mx0), (st1, mx1))

        def run(j, produce_next):
            for u in range(unroll):
                if u < unroll - 1 or produce_next:
                    produce(j + u + 1, *bufs[(u + 1) % 2])
                st_ref, mx_ref = bufs[u % 2]
                consume(st_ref[...], mx_ref[...], vl_ref[j + u])

        produce(0, st0, mx0)

        def body(jj, carry):
            run(unroll * jj, True)
            return carry

        lax.fori_loop(0, n_lat // unroll - 1, body, 0)
        run(n_lat - unroll, False)
    acc = acc_scr[...]
    o = (acc[:dv] / acc[dv:dv + 1]).T
    if group > 1:
        o = jnp.concatenate([o[g * tq:(g + 1) * tq] for g in range(group)], axis=1)
    o_ref[...] = o.astype(o_ref.dtype)


def _flash(q, kc, vc, kl, vl, group, tq, out_dtype):
    b, hq, _, s = q.shape
    hk, sc = kc.shape[1], kc.shape[2]
    hv, dvx = vc.shape[1], vc.shape[3]
    dv = dvx - V_EXT
    tq = min(tq, s)
    cols = group * tq
    n_lat = 0 if kl is None else vl.shape[2]
    in_specs = [pl.BlockSpec((None, group, HEAD, tq), lambda bi, h, i: (bi, h, 0, i)),
                pl.BlockSpec((None, None, sc, HEAD), lambda bi, h, i: (bi, h, 0, 0)),
                pl.BlockSpec((None, None, None, dvx, sc), lambda bi, h, i: (bi, h % hv, 0, 0, 0))]
    args = [q, kc, vc]
    scratch = [pltpu.VMEM((1, cols), F32), pltpu.VMEM((dvx, cols), F32)]
    if n_lat:
        assert n_lat % 2 == 0, "latent key chunks alternate between two score buffers"
        sl, tk = kl.shape[2], vl.shape[4]
        in_specs += [pl.BlockSpec((None, None, sl, HEAD), lambda bi, h, i: (bi, h, 0, 0)),
                     pl.BlockSpec((None, None, n_lat, dvx, tk), lambda bi, h, i: (bi, h % hv, 0, 0, 0))]
        args += [kl, vl]
        scratch += [pltpu.VMEM((tk, cols), F32), pltpu.VMEM((tk, cols), F32),
                    pltpu.VMEM((1, cols), F32), pltpu.VMEM((1, cols), F32)]
    unroll = next(u for u in (8, 4, 2) if n_lat % u == 0)
    kern = functools.partial(_flash_kernel, group=group, tq=tq, dv=dv, n_lat=n_lat, unroll=unroll)
    return pl.pallas_call(
        kern,
        out_shape=jax.ShapeDtypeStruct((b, s, hq * dv), out_dtype),
        grid=(b, hk, s // tq),
        in_specs=in_specs,
        out_specs=pl.BlockSpec((None, tq, group * dv), lambda bi, h, i: (bi, i, h)),
        scratch_shapes=scratch,
        compiler_params=_cparams(3),
        name="flash_g%d_l%d" % (group, n_lat),
    )(*args)


def _attn_out_kernel(od_ref, og_ref, lq1_ref, lk1_ref, lq2_ref, lk2_ref, subln_ref, w_ref,
                     x_ref, g1_ref, o_ref, *, lam_init):
    lam = (jnp.exp(jnp.sum(lq1_ref[...] * lk1_ref[...], axis=-1, keepdims=True))
           - jnp.exp(jnp.sum(lq2_ref[...] * lk2_ref[...], axis=-1, keepdims=True)) + lam_init)
    od = od_ref[...]
    half = N_DIFF * DIFF_V
    parts = []
    for hh in range(N_DIFF):
        dh = od[:, DIFF_V * hh:DIFF_V * (hh + 1)] - lam * od[:, half + DIFF_V * hh:half + DIFF_V * (hh + 1)]
        parts.append(_rms(dh, subln_ref[...]) * (1.0 - lam_init))
    att = jnp.concatenate([p.astype(BF16) for p in parts] + [og_ref[...]], axis=1)
    y = jnp.dot(att, w_ref[...], preferred_element_type=F32)
    o_ref[...] = x_ref[...] + g1_ref[...] * y


def _attn_out(od, og, lq1, lk1, lq2, lk2, subln, w_out, x, g1, lam_init, tile):
    b, s, d = x.shape
    t = min(tile, s)
    vec = pl.BlockSpec((None, 1, d), lambda bi, i: (bi, 0, 0))
    row = lambda a: pl.BlockSpec((None, t, a.shape[2]), lambda bi, i: (bi, i, 0))
    return pl.pallas_call(
        functools.partial(_attn_out_kernel, lam_init=lam_init),
        out_shape=jax.ShapeDtypeStruct(x.shape, F32),
        grid=(b, s // t),
        in_specs=[row(od), row(og), _const_spec(lq1.shape), _const_spec(lk1.shape),
                  _const_spec(lq2.shape), _const_spec(lk2.shape), _const_spec(subln.shape),
                  _const_spec(w_out.shape), row(x), vec],
        out_specs=row(x),
        compiler_params=_cparams(2),
        name="attn_out",
    )(od, og, lq1, lk1, lq2, lk2, subln, w_out, x, g1)


def _sgu_kernel(x_ref, sh_ref, sc_ref, g1_ref, nw_ref, win_ref, vn_ref, ws_ref, bs_ref, wout_ref, o_ref):
    x = x_ref[...]
    t, d = x.shape
    h = (_rms(x, nw_ref[...]) * (1.0 + sc_ref[...]) + sh_ref[...]).astype(BF16)
    z = jnp.dot(h, win_ref[...], preferred_element_type=F32)
    z = 0.5 * z * (1.0 + lax.erf(z * (2.0 ** -0.5)))
    sd = z.shape[1] // 2
    u = z[:, :sd]
    v = _rms(z[:, sd:], vn_ref[...]).astype(BF16)
    gw = sd // SGU_GROUPS
    rows = []
    for n in range(t // SGU_CHUNK):
        cols = []
        for g in range(SGU_GROUPS):
            vg = v[SGU_CHUNK * n:SGU_CHUNK * (n + 1), gw * g:gw * (g + 1)]
            cols.append(jnp.dot(ws_ref[g], vg, preferred_element_type=F32))
        rows.append(jnp.concatenate(cols, axis=1) + bs_ref[...])
    mixed = jnp.concatenate(rows, axis=0) if len(rows) > 1 else rows[0]
    y = jnp.dot((u * mixed).astype(BF16), wout_ref[...], preferred_element_type=F32)
    o_ref[...] = x + g1_ref[...] * y


def _sgu(x, sh, sc, g1, nw, w_in, vn, w_s, bs_full, w_out, tile):
    b, s, d = x.shape
    t = min(tile, s)
    vec = pl.BlockSpec((None, 1, d), lambda bi, i: (bi, 0, 0))
    row = pl.BlockSpec((None, t, d), lambda bi, i: (bi, i, 0))
    return pl.pallas_call(
        _sgu_kernel,
        out_shape=jax.ShapeDtypeStruct(x.shape, F32),
        grid=(b, s // t),
        in_specs=[row, vec, vec, vec, _const_spec(nw.shape), _const_spec(w_in.shape),
                  _const_spec(vn.shape), _const_spec(w_s.shape), _const_spec(bs_full.shape),
                  _const_spec(w_out.shape)],
        out_specs=row,
        compiler_params=_cparams(2),
        name="sgu",
    )(x, sh, sc, g1, nw, w_in, vn, w_s, bs_full, w_out)


def _ffn_kernel(xp_ref, x_ref, xn_ref, sh_ref, sc_ref, g2_ref, nw_ref, wup_ref, cw_ref, cb_ref,
                wdn_ref, fn_ref, o_ref, z_scr, *, final_norm):
    i = pl.program_id(1)
    last = pl.num_programs(1) - 1
    x = x_ref[...]
    t = x.shape[0]
    xa = jnp.concatenate([xp_ref[...], x, xn_ref[...]], axis=0)
    h = _rms(xa, nw_ref[...]) * (1.0 + sc_ref[...]) + sh_ref[...]
    r = lax.broadcasted_iota(jnp.int32, (t + 2 * CONV_HALO, 1), 0)
    outside = ((r < CONV_HALO) & (i == 0)) | ((r >= t + CONV_HALO) & (i == last))
    h = jnp.where(outside, 0.0, h).astype(BF16)
    z_scr[...] = jnp.dot(h, wup_ref[...], preferred_element_type=F32)
    cw = cw_ref[...]
    zc = (cw[0:1] * z_scr[pl.ds(CONV_HALO - 1, t), :] + cw[1:2] * z_scr[pl.ds(CONV_HALO, t), :]
          + cw[2:3] * z_scr[pl.ds(CONV_HALO + 1, t), :] + cb_ref[...])
    f = zc.shape[1] // 2
    g = zc[:, :f]
    act = (g * _sigmoid(g) * zc[:, f:]).astype(BF16)
    y = jnp.dot(act, wdn_ref[...], preferred_element_type=F32)
    out = x + g2_ref[...] * y
    if final_norm:
        out = _rms(out, fn_ref[...])
    o_ref[...] = out


def _ffn(x, sh, sc, g2, nw, w_up, conv_w, conv_b, w_dn, fn, final_norm, tile):
    b, s, d = x.shape
    t = min(tile, s)
    tb = t // CONV_HALO
    nhb = s // CONV_HALO
    vec = pl.BlockSpec((None, 1, d), lambda bi, i: (bi, 0, 0))
    row = pl.BlockSpec((None, t, d), lambda bi, i: (bi, i, 0))
    prev = pl.BlockSpec((None, CONV_HALO, d), lambda bi, i: (bi, jnp.maximum(i * tb - 1, 0), 0))
    nxt = pl.BlockSpec((None, CONV_HALO, d), lambda bi, i: (bi, jnp.minimum((i + 1) * tb, nhb - 1), 0))
    return pl.pallas_call(
        functools.partial(_ffn_kernel, final_norm=final_norm),
        out_shape=jax.ShapeDtypeStruct(x.shape, F32),
        grid=(b, s // t),
        in_specs=[prev, row, nxt, vec, vec, vec, _const_spec(nw.shape), _const_spec(w_up.shape),
                  _const_spec(conv_w.shape), _const_spec(conv_b.shape), _const_spec(w_dn.shape),
                  _const_spec(fn.shape)],
        out_specs=row,
        scratch_shapes=[pltpu.VMEM((t + 2 * CONV_HALO, w_up.shape[1]), F32)],
        compiler_params=_cparams(2),
        name="ffn",
    )(x, x, x, sh, sc, g2, nw, w_up, conv_w, conv_b, w_dn, fn)


def _pair_swap_vec(g):
    return g.reshape(-1, 2)[:, ::-1].reshape(-1)


def _rope_tables(rows):
    n_freq = HEAD // 4
    inv = ROPE_THETA ** (-jnp.arange(n_freq, dtype=F32) / n_freq)
    row = jnp.repeat(jnp.arange(rows, dtype=F32), GRID_COLS)
    col = jnp.tile(jnp.arange(GRID_COLS, dtype=F32), rows)
    ang = jnp.concatenate([row[:, None] * inv, col[:, None] * inv], axis=-1)
    expand = lambda t: jnp.tile(jnp.repeat(t, 2, axis=-1), (1, N_DIFF))
    return expand(jnp.cos(ang)), expand(jnp.sin(ang))


def kernel(x, c, ctx, c_ctx, ada_w, ada_b, mix_norm, ffn_norm, final_norm, attn_w_in, attn_w_out,
           diff_lq1, diff_lk1, diff_lq2, diff_lk2, diff_subln, gqa_q_norm, gqa_k_norm, sgu_w_in,
           sgu_v_norm, sgu_w_s, sgu_b_s, sgu_w_out, ffn_w_up, ffn_conv_w, ffn_conv_b, ffn_w_down):
    b, s, d = x.shape
    n_ctx = ctx.shape[1]
    depth = ada_w.shape[0]
    last_attn = (depth - 1) // 2 * 2

    rows_pad = -(-(b + 1) // SUBLANES) * SUBLANES
    cvec = jnp.zeros((rows_pad, d), F32).at[:b].set(c).at[b].set(c_ctx)
    mods = _mod_vectors(cvec, ada_w, ada_b)

    cos_lat, sin_lat = _rope_tables(s // GRID_COLS)
    cos_ctx = jnp.ones((n_ctx, 4 * HEAD), F32)
    sin_ctx = jnp.zeros((n_ctx, 4 * HEAD), F32)
    eye = jnp.repeat(jnp.repeat(jnp.eye(N_GQ, dtype=BF16), HEAD, axis=0), HEAD, axis=1)
    fn = final_norm.reshape(1, d)

    def split_mod(m):
        return [m[:, k * d:(k + 1) * d].reshape(b, 1, d) for k in range(N_MOD)]

    for l in range(depth):
        i = l // 2
        is_attn = l % 2 == 0
        update_ctx = l < last_attn
        sh1, sc1, g1, sh2, sc2, g2 = split_mod(mods[l, :b])
        need_ctx = is_attn or update_ctx
        if need_ctx:
            csh1, csc1, cg1, csh2, csc2, cg2 = split_mod(jnp.broadcast_to(mods[l, b], (b, N_MOD * d)))
        nw1 = mix_norm[l].reshape(1, d)
        nw2 = ffn_norm[l].reshape(1, d)

        if is_attn:
            lam_init = 0.8 - 0.6 * math.exp(-0.3 * l)
            w = attn_w_in[i]
            w_qkv = w.astype(BF16)
            gq = jnp.tile(gqa_q_norm[i], N_GQ).reshape(1, -1)
            gqs = jnp.tile(_pair_swap_vec(gqa_q_norm[i]), N_GQ).reshape(1, -1)
            gk = jnp.tile(gqa_k_norm[i], N_GKV).reshape(1, -1)
            gks = jnp.tile(_pair_swap_vec(gqa_k_norm[i]), N_GKV).reshape(1, -1)
            w_out = attn_w_out[i].astype(BF16)
            vecs = [v[i].reshape(1, -1) for v in (diff_lq1, diff_lk1, diff_lq2, diff_lk2, diff_subln)]

            lat = _attn_proj(x, sh1, sc1, nw1, w_qkv, cos_lat, sin_lat, gq, gqs, gk, gks, eye, PROJ_TILE)
            cpr = _attn_proj(ctx, csh1, csc1, nw1, w_qkv, cos_ctx, sin_ctx, gq, gqs, gk, gks, eye, n_ctx)
            od = _flash(lat[0], cpr[1], cpr[2], lat[1], lat[2], 1, FLASH_COLS, F32)
            og = _flash(lat[3], cpr[4], cpr[5], lat[4], lat[5], GQ_GROUP, FLASH_COLS // GQ_GROUP, BF16)
            x_new = _attn_out(od, og, *vecs, w_out, x, g1, lam_init, ATTN_OUT_TILE)
            if update_ctx:
                cod = _flash(cpr[0], cpr[1], cpr[2], None, None, 1, FLASH_COLS, F32)
                cog = _flash(cpr[3], cpr[4], cpr[5], None, None, GQ_GROUP, FLASH_COLS // GQ_GROUP, BF16)
                ctx = _attn_out(cod, cog, *vecs, w_out, ctx, cg1, lam_init, ATTN_OUT_TILE)
            x = x_new
        else:
            w_in = sgu_w_in[i].astype(BF16)
            w_s = sgu_w_s[i].astype(BF16)
            w_out = sgu_w_out[i].astype(BF16)
            vn = sgu_v_norm[i].reshape(1, -1)
            gw = sgu_w_out.shape[1] // SGU_GROUPS
            bs_full = jnp.repeat(sgu_b_s[i].T, gw, axis=1)
            x_new = _sgu(x, sh1, sc1, g1, nw1, w_in, vn, w_s, bs_full, w_out, SGU_TILE)
            if update_ctx:
                ctx = _sgu(ctx, csh1, csc1, cg1, nw1, w_in, vn, w_s, bs_full, w_out, SGU_TILE)
            x = x_new

        w_up = ffn_w_up[l].astype(BF16)
        w_dn = ffn_w_down[l].astype(BF16)
        cb = ffn_conv_b[l].reshape(1, -1)
        x = _ffn(x, sh2, sc2, g2, nw2, w_up, ffn_conv_w[l], cb, w_dn, fn, l == depth - 1, FFN_TILE)
        if update_ctx:
            ctx = _ffn(ctx, csh2, csc2, cg2, nw2, w_up, ffn_conv_w[l], cb, w_dn, fn, False, FFN_TILE)
    return x
```

```python
import functools
import math

import jax
import jax.numpy as jnp
from jax import lax
from jax.experimental import pallas as pl
from jax.experimental.pallas import tpu as pltpu

F32 = jnp.float32
BF16 = jnp.bfloat16

EPS = 1e-6
HEAD = 64
N_DIFF = 4
DIFF_V = 2 * HEAD
N_GQ = 8
N_GKV = 2
GQ_GROUP = N_GQ // N_GKV
ROPE_THETA = 10000.0
GRID_COLS = 64
N_MOD = 6
SGU_CHUNK = 128
SGU_GROUPS = 4
SUBLANES = 8
CONV_HALO = SUBLANES
V_EXT = 2 * SUBLANES
MOD_TILE_N = 1536
PROJ_TILE = 512
FLASH_COLS = 1024
ATTN_OUT_TILE = 1024
SGU_TILE = 512
FFN_TILE = 512
QK_SCALE = HEAD ** -0.5 * math.log2(math.e)
VMEM_LIMIT = 56 * 1024 * 1024
BOUND_SLACK = 1.0 + 2.0 ** -6
MIN_DENOM = 2.0 ** -80

QK_A = N_DIFF * HEAD
C_Q1, C_Q2, C_K1, C_K2 = 0, QK_A, 2 * QK_A, 3 * QK_A
C_VA = 4 * QK_A
C_QB = C_VA + N_DIFF * DIFF_V
C_KB = C_QB + N_GQ * HEAD
C_VB = C_KB + N_GKV * HEAD
C_END = C_VB + N_GKV * HEAD


def _cparams(n_axes):
    return pltpu.CompilerParams(dimension_semantics=("arbitrary",) * n_axes,
                                vmem_limit_bytes=VMEM_LIMIT)


def _const_spec(shape):
    nd = len(shape)
    return pl.BlockSpec(shape, lambda *_: (0,) * nd, pipeline_mode=pl.Buffered(1))


def _rms(xf, g):
    ms = jnp.mean(xf * xf, axis=-1, keepdims=True)
    return xf * lax.rsqrt(ms + EPS) * g


def _sigmoid(x):
    return 1.0 / (1.0 + jnp.exp(-x))


def _mod_kernel(c_ref, w_ref, b_ref, o_ref):
    c = c_ref[...]
    s = c * _sigmoid(c)
    o_ref[...] = jnp.dot(s, w_ref[...], precision=lax.Precision.HIGHEST,
                         preferred_element_type=F32) + b_ref[...]


def _mod_vectors(cvec, ada_w, ada_b):
    depth, d, n = ada_w.shape
    rows = cvec.shape[0]
    tn = MOD_TILE_N
    return pl.pallas_call(
        _mod_kernel,
        out_shape=jax.ShapeDtypeStruct((depth, rows, n), F32),
        grid=(depth, n // tn),
        in_specs=[pl.BlockSpec((rows, d), lambda l, j: (0, 0)),
                  pl.BlockSpec((None, d, tn), lambda l, j: (l, 0, j)),
                  pl.BlockSpec((None, 1, tn), lambda l, j: (l, 0, j))],
        out_specs=pl.BlockSpec((None, rows, tn), lambda l, j: (l, 0, j)),
        compiler_params=_cparams(2),
        name="mod_vectors",
    )(cvec, ada_w, ada_b.reshape(depth, 1, n))


def _group_sumsq(a, e):
    sq = a * a
    hi = sq.astype(BF16)
    lo = (sq - hi.astype(F32)).astype(BF16)
    return (jnp.dot(hi, e, preferred_element_type=F32) + jnp.dot(lo, e, preferred_element_type=F32))


def _pair_partner(a):
    n = a.shape[1]
    even = lax.broadcasted_iota(jnp.int32, (1, n), 1) % 2 == 0
    return jnp.where(even, -pltpu.roll(a, n - 1, axis=1), pltpu.roll(a, 1, axis=1))


def _attn_proj_kernel(x_ref, sh_ref, sc_ref, nw_ref, w_ref, cos_ref, sin_ref,
                      gq_ref, gqs_ref, gk_ref, gks_ref, e_ref,
                      qd_ref, kd_ref, va_ref, qg_ref, kg_ref, vg_ref, kn_ref):
    x = x_ref[...]
    h = (_rms(x, nw_ref[...]) * (1.0 + sc_ref[...]) + sh_ref[...]).astype(BF16)
    a = jnp.dot(h, w_ref[...], preferred_element_type=F32)
    cos = cos_ref[...]
    sin = sin_ref[...]
    scale = QK_SCALE

    for j in range(4):
        aj = a[:, C_Q1 + QK_A * j:C_Q1 + QK_A * (j + 1)]
        r = aj * cos + _pair_partner(aj) * sin
        if j < 2:
            rt = (r * scale).T.astype(BF16)
            for hh in range(N_DIFF):
                qd_ref[j * N_DIFF + hh] = rt[HEAD * hh:HEAD * (hh + 1), :]
        else:
            r = r.astype(BF16)
            for hh in range(N_DIFF):
                kd_ref[(j - 2) * N_DIFF + hh] = r[:, HEAD * hh:HEAD * (hh + 1)]
            ssq = _group_sumsq(r.astype(F32), e_ref[:QK_A, :QK_A])
            kn_ref[:, QK_A * (j - 2):QK_A * (j - 1)] = jnp.max(ssq, axis=0, keepdims=True)
    t = x.shape[0]
    ext = (lax.broadcasted_iota(jnp.int32, (V_EXT, t), 0) == 0).astype(BF16)
    for hh in range(N_DIFF):
        va_ref[hh, 0, :DIFF_V, :] = a[:, C_VA + DIFF_V * hh:C_VA + DIFF_V * (hh + 1)].T.astype(BF16)
        va_ref[hh, 0, DIFF_V:, :] = ext

    e = e_ref[...]
    cos2 = jnp.concatenate([cos, cos], axis=1)
    sin2 = jnp.concatenate([sin, sin], axis=1)
    qb = a[:, C_QB:C_KB]
    rq = lax.rsqrt(_group_sumsq(qb, e) * (1.0 / HEAD) + EPS)
    qt = (rq * (qb * (gq_ref[...] * cos2) + _pair_partner(qb) * (gqs_ref[...] * sin2)) * scale).T.astype(BF16)
    for hh in range(N_GQ):
        qg_ref[hh] = qt[HEAD * hh:HEAD * (hh + 1), :]
    nk = N_GKV * HEAD
    kb = a[:, C_KB:C_VB]
    rk = lax.rsqrt(_group_sumsq(kb, e[:nk, :nk]) * (1.0 / HEAD) + EPS)
    k = (rk * (kb * (gk_ref[...] * cos[:, :nk]) + _pair_partner(kb) * (gks_ref[...] * sin[:, :nk]))).astype(BF16)
    kn_ref[:, 2 * QK_A:] = jnp.max(_group_sumsq(k.astype(F32), e[:nk, :nk]), axis=0, keepdims=True)
    vbt = a[:, C_VB:C_END].T.astype(BF16)
    for hh in range(N_GKV):
        kg_ref[hh] = k[:, HEAD * hh:HEAD * (hh + 1)]
        vg_ref[hh, 0, :HEAD, :] = vbt[HEAD * hh:HEAD * (hh + 1), :]
        vg_ref[hh, 0, HEAD:, :] = ext


def _attn_proj(x, sh, sc, nw, w_qkv, cos, sin, gq, gqs, gk, gks, e, tile):
    b, s, d = x.shape
    t = min(tile, s)
    hd = lambda n, w: jax.ShapeDtypeStruct((b, n, s, w), BF16)
    hspec = lambda n, w: pl.BlockSpec((None, n, t, w), lambda bi, i: (bi, 0, i, 0))
    qd = lambda n: jax.ShapeDtypeStruct((b, n, HEAD, s), BF16)
    qspec = lambda n: pl.BlockSpec((None, n, HEAD, t), lambda bi, i: (bi, 0, 0, i))
    vd = lambda n, w: jax.ShapeDtypeStruct((b, n, s // t, w + V_EXT, t), BF16)
    vspec = lambda n, w: pl.BlockSpec((None, n, 1, w + V_EXT, t), lambda bi, i: (bi, 0, i, 0, 0))
    vec = pl.BlockSpec((None, 1, d), lambda bi, i: (bi, 0, 0))
    n_kn = (2 * N_DIFF + N_GKV) * HEAD
    return pl.pallas_call(
        _attn_proj_kernel,
        out_shape=(qd(2 * N_DIFF), hd(2 * N_DIFF, HEAD), vd(N_DIFF, DIFF_V),
                   qd(N_GQ), hd(N_GKV, HEAD), vd(N_GKV, HEAD),
                   jax.ShapeDtypeStruct((b, s // t, 1, n_kn), F32)),
        grid=(b, s // t),
        in_specs=[pl.BlockSpec((None, t, d), lambda bi, i: (bi, i, 0)), vec, vec,
                  _const_spec((1, d)), _const_spec(w_qkv.shape),
                  pl.BlockSpec((t, 4 * HEAD), lambda bi, i: (i, 0)),
                  pl.BlockSpec((t, 4 * HEAD), lambda bi, i: (i, 0)),
                  _const_spec(gq.shape), _const_spec(gqs.shape),
                  _const_spec(gk.shape), _const_spec(gks.shape), _const_spec(e.shape)],
        out_specs=(qspec(2 * N_DIFF), hspec(2 * N_DIFF, HEAD), vspec(N_DIFF, DIFF_V),
                   qspec(N_GQ), hspec(N_GKV, HEAD), vspec(N_GKV, HEAD),
                   pl.BlockSpec((None, None, 1, n_kn), lambda bi, i: (bi, i, 0, 0))),
        compiler_params=_cparams(2),
        name="attn_proj",
    )(x, sh, sc, nw, w_qkv, cos, sin, gq, gqs, gk, gks, e)


def _flash_kernel(*refs, group, tq, dv, n_lat, unroll):
    if n_lat:
        q_ref, kn_ref, kc_ref, vc_ref, kl_ref, vl_ref, o_ref, m_scr, acc_scr, st0, st1, mx0, mx1 = refs
    else:
        q_ref, kn_ref, kc_ref, vc_ref, o_ref, m_scr, acc_scr = refs
    cols = group * tq
    qt = q_ref[0] if group == 1 else jnp.concatenate([q_ref[g] for g in range(group)], axis=1)

    def finalize():
        acc = acc_scr[...]
        o = (acc[:dv] / acc[dv:dv + 1]).T
        if group > 1:
            o = jnp.concatenate([o[g * tq:(g + 1) * tq] for g in range(group)], axis=1)
        o_ref[...] = o.astype(o_ref.dtype)

    qf = qt.astype(F32)
    bound = jnp.sqrt(jnp.sum(qf * qf, axis=0, keepdims=True) * kn_ref[:, :1]) * BOUND_SLACK
    acc_scr[...] = jnp.zeros(acc_scr.shape, F32)

    def add(k, vt):
        st = jnp.dot(k, qt, preferred_element_type=F32)
        acc_scr[...] += jnp.dot(vt, jnp.exp2(st - bound).astype(BF16), preferred_element_type=F32)

    add(kc_ref[...], vc_ref[...])
    if n_lat:
        tk = vl_ref.shape[2]

        def fast_body(jj, carry):
            for u in range(unroll):
                j = unroll * jj + u
                start = pl.multiple_of(j * tk, tk)
                add(kl_ref[pl.ds(start, tk), :], vl_ref[j])
            return carry

        lax.fori_loop(0, n_lat // unroll, fast_body, 0)
    ok = jnp.min(acc_scr[dv:dv + 1, :]) > MIN_DENOM

    def scores(k):
        st = jnp.dot(k, qt, preferred_element_type=F32)
        return st, jnp.max(st, axis=0, keepdims=True)

    def consume(st, mx, vt):
        m_old = m_scr[...]
        m_new = jnp.maximum(m_old, mx)
        alpha = jnp.exp2(m_old - m_new)
        pt = jnp.exp2(st - m_new).astype(BF16)
        acc_scr[...] = alpha * acc_scr[...] + jnp.dot(vt, pt, preferred_element_type=F32)
        m_scr[...] = m_new

    def exact():
        m_scr[...] = jnp.full(m_scr.shape, -jnp.inf, F32)
        acc_scr[...] = jnp.zeros(acc_scr.shape, F32)
        consume(*scores(kc_ref[...]), vc_ref[...])
        if n_lat:
            def produce(j, st_ref, mx_ref):
                start = pl.multiple_of(j * tk, tk)
                st_ref[...], mx_ref[...] = scores(kl_ref[pl.ds(start, tk), :])

            bufs = ((st0, mx0), (st1, mx1))

            def run(j, produce_next):
                for u in range(unroll):
                    if u < unroll - 1 or produce_next:
                        produce(j + u + 1, *bufs[(u + 1) % 2])
                    st_ref, mx_ref = bufs[u % 2]
                    consume(st_ref[...], mx_ref[...], vl_ref[j + u])

            produce(0, st0, mx0)

            def body(jj, carry):
                run(unroll * jj, True)
                return carry

            lax.fori_loop(0, n_lat // unroll - 1, body, 0)
            run(n_lat - unroll, False)

    @pl.when(jnp.logical_not(ok))
    def _():
        exact()

    finalize()


def _flash(q, kn, kc, vc, kl, vl, group, tq, out_dtype):
    b, hq, _, s = q.shape
    hk, sc = kc.shape[1], kc.shape[2]
    hv, dvx = vc.shape[1], vc.shape[3]
    dv = dvx - V_EXT
    tq = min(tq, s)
    cols = group * tq
    n_lat = 0 if kl is None else vl.shape[2]
    in_specs = [pl.BlockSpec((None, group, HEAD, tq), lambda bi, h, i: (bi, h, 0, i)),
                pl.BlockSpec((None, None, 1, 2 * HEAD), lambda bi, h, i: (bi, h, 0, 0)),
                pl.BlockSpec((None, None, sc, HEAD), lambda bi, h, i: (bi, h, 0, 0)),
                pl.BlockSpec((None, None, None, dvx, sc), lambda bi, h, i: (bi, h % hv, 0, 0, 0))]
    args = [q, kn, kc, vc]
    scratch = [pltpu.VMEM((1, cols), F32), pltpu.VMEM((dvx, cols), F32)]
    if n_lat:
        assert n_lat % 2 == 0, "latent key chunks alternate between two score buffers"
        sl, tk = kl.shape[2], vl.shape[4]
        in_specs += [pl.BlockSpec((None, None, sl, HEAD), lambda bi, h, i: (bi, h, 0, 0)),
                     pl.BlockSpec((None, None, n_lat, dvx, tk), lambda bi, h, i: (bi, h % hv, 0, 0, 0))]
        args += [kl, vl]
        scratch += [pltpu.VMEM((tk, cols), F32), pltpu.VMEM((tk, cols), F32),
                    pltpu.VMEM((1, cols), F32), pltpu.VMEM((1, cols), F32)]
    unroll = next(u for u in (8, 4, 2) if n_lat % u == 0)
    kern = functools.partial(_flash_kernel, group=group, tq=tq, dv=dv, n_lat=n_lat, unroll=unroll)
    return pl.pallas_call(
        kern,
        out_shape=jax.ShapeDtypeStruct((b, s, hq * dv), out_dtype),
        grid=(b, hk, s // tq),
        in_specs=in_specs,
        out_specs=pl.BlockSpec((None, tq, group * dv), lambda bi, h, i: (bi, i, h)),
        scratch_shapes=scratch,
        compiler_params=_cparams(3),
        name="flash_g%d_l%d" % (group, n_lat),
    )(*args)


def _attn_out_kernel(od_ref, og_ref, lq1_ref, lk1_ref, lq2_ref, lk2_ref, subln_ref, w_ref,
                     x_ref, g1_ref, o_ref, *, lam_init):
    lam = (jnp.exp(jnp.sum(lq1_ref[...] * lk1_ref[...], axis=-1, keepdims=True))
           - jnp.exp(jnp.sum(lq2_ref[...] * lk2_ref[...], axis=-1, keepdims=True)) + lam_init)
    od = od_ref[...]
    half = N_DIFF * DIFF_V
    parts = []
    for hh in range(N_DIFF):
        dh = od[:, DIFF_V * hh:DIFF_V * (hh + 1)] - lam * od[:, half + DIFF_V * hh:half + DIFF_V * (hh + 1)]
        parts.append(_rms(dh, subln_ref[...]) * (1.0 - lam_init))
    att = jnp.concatenate([p.astype(BF16) for p in parts] + [og_ref[...]], axis=1)
    y = jnp.dot(att, w_ref[...], preferred_element_type=F32)
    o_ref[...] = x_ref[...] + g1_ref[...] * y


def _attn_out(od, og, lq1, lk1, lq2, lk2, subln, w_out, x, g1, lam_init, tile):
    b, s, d = x.shape
    t = min(tile, s)
    vec = pl.BlockSpec((None, 1, d), lambda bi, i: (bi, 0, 0))
    row = lambda a: pl.BlockSpec((None, t, a.shape[2]), lambda bi, i: (bi, i, 0))
    return pl.pallas_call(
        functools.partial(_attn_out_kernel, lam_init=lam_init),
        out_shape=jax.ShapeDtypeStruct(x.shape, F32),
        grid=(b, s // t),
        in_specs=[row(od), row(og), _const_spec(lq1.shape), _const_spec(lk1.shape),
                  _const_spec(lq2.shape), _const_spec(lk2.shape), _const_spec(subln.shape),
                  _const_spec(w_out.shape), row(x), vec],
        out_specs=row(x),
        compiler_params=_cparams(2),
        name="attn_out",
    )(od, og, lq1, lk1, lq2, lk2, subln, w_out, x, g1)


def _sgu_kernel(x_ref, sh_ref, sc_ref, g1_ref, nw_ref, win_ref, vn_ref, ws_ref, bs_ref, wout_ref, o_ref):
    x = x_ref[...]
    t, d = x.shape
    h = (_rms(x, nw_ref[...]) * (1.0 + sc_ref[...]) + sh_ref[...]).astype(BF16)
    z = jnp.dot(h, win_ref[...], preferred_element_type=F32)
    z = 0.5 * z * (1.0 + lax.erf(z * (2.0 ** -0.5)))
    sd = z.shape[1] // 2
    u = z[:, :sd]
    v = _rms(z[:, sd:], vn_ref[...]).astype(BF16)
    gw = sd // SGU_GROUPS
    rows = []
    for n in range(t // SGU_CHUNK):
        cols = []
        for g in range(SGU_GROUPS):
            vg = v[SGU_CHUNK * n:SGU_CHUNK * (n + 1), gw * g:gw * (g + 1)]
            cols.append(jnp.dot(ws_ref[g], vg, preferred_element_type=F32))
        rows.append(jnp.concatenate(cols, axis=1) + bs_ref[...])
    mixed = jnp.concatenate(rows, axis=0) if len(rows) > 1 else rows[0]
    y = jnp.dot((u * mixed).astype(BF16), wout_ref[...], preferred_element_type=F32)
    o_ref[...] = x + g1_ref[...] * y


def _sgu(x, sh, sc, g1, nw, w_in, vn, w_s, bs_full, w_out, tile):
    b, s, d = x.shape
    t = min(tile, s)
    vec = pl.BlockSpec((None, 1, d), lambda bi, i: (bi, 0, 0))
    row = pl.BlockSpec((None, t, d), lambda bi, i: (bi, i, 0))
    return pl.pallas_call(
        _sgu_kernel,
        out_shape=jax.ShapeDtypeStruct(x.shape, F32),
        grid=(b, s // t),
        in_specs=[row, vec, vec, vec, _const_spec(nw.shape), _const_spec(w_in.shape),
                  _const_spec(vn.shape), _const_spec(w_s.shape), _const_spec(bs_full.shape),
                  _const_spec(w_out.shape)],
        out_specs=row,
        compiler_params=_cparams(2),
        name="sgu",
    )(x, sh, sc, g1, nw, w_in, vn, w_s, bs_full, w_out)


def _ffn_kernel(xp_ref, x_ref, xn_ref, sh_ref, sc_ref, g2_ref, nw_ref, wup_ref, cw_ref, cb_ref,
                wdn_ref, fn_ref, o_ref, z_scr, *, final_norm):
    i = pl.program_id(1)
    last = pl.num_programs(1) - 1
    x = x_ref[...]
    t = x.shape[0]
    xa = jnp.concatenate([xp_ref[...], x, xn_ref[...]], axis=0)
    h = _rms(xa, nw_ref[...]) * (1.0 + sc_ref[...]) + sh_ref[...]
    r = lax.broadcasted_iota(jnp.int32, (t + 2 * CONV_HALO, 1), 0)
    outside = ((r < CONV_HALO) & (i == 0)) | ((r >= t + CONV_HALO) & (i == last))
    h = jnp.where(outside, 0.0, h).astype(BF16)
    z_scr[...] = jnp.dot(h, wup_ref[...], preferred_element_type=F32)
    cw = cw_ref[...]
    zc = (cw[0:1] * z_scr[pl.ds(CONV_HALO - 1, t), :] + cw[1:2] * z_scr[pl.ds(CONV_HALO, t), :]
          + cw[2:3] * z_scr[pl.ds(CONV_HALO + 1, t), :] + cb_ref[...])
    f = zc.shape[1] // 2
    g = zc[:, :f]
    act = (g * _sigmoid(g) * zc[:, f:]).astype(BF16)
    y = jnp.dot(act, wdn_ref[...], preferred_element_type=F32)
    out = x + g2_ref[...] * y
    if final_norm:
        out = _rms(out, fn_ref[...])
    o_ref[...] = out


def _ffn(x, sh, sc, g2, nw, w_up, conv_w, conv_b, w_dn, fn, final_norm, tile):
    b, s, d = x.shape
    t = min(tile, s)
    tb = t // CONV_HALO
    nhb = s // CONV_HALO
    vec = pl.BlockSpec((None, 1, d), lambda bi, i: (bi, 0, 0))
    row = pl.BlockSpec((None, t, d), lambda bi, i: (bi, i, 0))
    prev = pl.BlockSpec((None, CONV_HALO, d), lambda bi, i: (bi, jnp.maximum(i * tb - 1, 0), 0))
    nxt = pl.BlockSpec((None, CONV_HALO, d), lambda bi, i: (bi, jnp.minimum((i + 1) * tb, nhb - 1), 0))
    return pl.pallas_call(
        functools.partial(_ffn_kernel, final_norm=final_norm),
        out_shape=jax.ShapeDtypeStruct(x.shape, F32),
        grid=(b, s // t),
        in_specs=[prev, row, nxt, vec, vec, vec, _const_spec(nw.shape), _const_spec(w_up.shape),
                  _const_spec(conv_w.shape), _const_spec(conv_b.shape), _const_spec(w_dn.shape),
                  _const_spec(fn.shape)],
        out_specs=row,
        scratch_shapes=[pltpu.VMEM((t + 2 * CONV_HALO, w_up.shape[1]), F32)],
        compiler_params=_cparams(2),
        name="ffn",
    )(x, x, x, sh, sc, g2, nw, w_up, conv_w, conv_b, w_dn, fn)


def _pair_swap_vec(g):
    return g.reshape(-1, 2)[:, ::-1].reshape(-1)


def _rope_tables(rows):
    n_freq = HEAD // 4
    inv = ROPE_THETA ** (-jnp.arange(n_freq, dtype=F32) / n_freq)
    row = jnp.repeat(jnp.arange(rows, dtype=F32), GRID_COLS)
    col = jnp.tile(jnp.arange(GRID_COLS, dtype=F32), rows)
    ang = jnp.concatenate([row[:, None] * inv, col[:, None] * inv], axis=-1)
    expand = lambda t: jnp.tile(jnp.repeat(t, 2, axis=-1), (1, N_DIFF))
    return expand(jnp.cos(ang)), expand(jnp.sin(ang))


def kernel(x, c, ctx, c_ctx, ada_w, ada_b, mix_norm, ffn_norm, final_norm, attn_w_in, attn_w_out,
           diff_lq1, diff_lk1, diff_lq2, diff_lk2, diff_subln, gqa_q_norm, gqa_k_norm, sgu_w_in,
           sgu_v_norm, sgu_w_s, sgu_b_s, sgu_w_out, ffn_w_up, ffn_conv_w, ffn_conv_b, ffn_w_down):
    b, s, d = x.shape
    n_ctx = ctx.shape[1]
    depth = ada_w.shape[0]
    last_attn = (depth - 1) // 2 * 2

    rows_pad = -(-(b + 1) // SUBLANES) * SUBLANES
    cvec = jnp.zeros((rows_pad, d), F32).at[:b].set(c).at[b].set(c_ctx)
    mods = _mod_vectors(cvec, ada_w, ada_b)

    cos_lat, sin_lat = _rope_tables(s // GRID_COLS)
    cos_ctx = jnp.ones((n_ctx, 4 * HEAD), F32)
    sin_ctx = jnp.zeros((n_ctx, 4 * HEAD), F32)
    eye = jnp.repeat(jnp.repeat(jnp.eye(N_GQ, dtype=BF16), HEAD, axis=0), HEAD, axis=1)
    fn = final_norm.reshape(1, d)

    def split_mod(m):
        return [m[:, k * d:(k + 1) * d].reshape(b, 1, d) for k in range(N_MOD)]

    for l in range(depth):
        i = l // 2
        is_attn = l % 2 == 0
        update_ctx = l < last_attn
        sh1, sc1, g1, sh2, sc2, g2 = split_mod(mods[l, :b])
        need_ctx = is_attn or update_ctx
        if need_ctx:
            csh1, csc1, cg1, csh2, csc2, cg2 = split_mod(jnp.broadcast_to(mods[l, b], (b, N_MOD * d)))
        nw1 = mix_norm[l].reshape(1, d)
        nw2 = ffn_norm[l].reshape(1, d)

        if is_attn:
            lam_init = 0.8 - 0.6 * math.exp(-0.3 * l)
            w = attn_w_in[i]
            w_qkv = w.astype(BF16)
            gq = jnp.tile(gqa_q_norm[i], N_GQ).reshape(1, -1)
            gqs = jnp.tile(_pair_swap_vec(gqa_q_norm[i]), N_GQ).reshape(1, -1)
            gk = jnp.tile(gqa_k_norm[i], N_GKV).reshape(1, -1)
            gks = jnp.tile(_pair_swap_vec(gqa_k_norm[i]), N_GKV).reshape(1, -1)
            w_out = attn_w_out[i].astype(BF16)
            vecs = [v[i].reshape(1, -1) for v in (diff_lq1, diff_lk1, diff_lq2, diff_lk2, diff_subln)]

            lat = _attn_proj(x, sh1, sc1, nw1, w_qkv, cos_lat, sin_lat, gq, gqs, gk, gks, eye, PROJ_TILE)
            cpr = _attn_proj(ctx, csh1, csc1, nw1, w_qkv, cos_ctx, sin_ctx, gq, gqs, gk, gks, eye, n_ctx)
            kn = jnp.maximum(jnp.max(lat[6], axis=1), jnp.max(cpr[6], axis=1))[:, 0]
            per_head = lambda v: jnp.broadcast_to(v.reshape(b, -1, HEAD)[:, :, :1, None],
                                                  (b, v.shape[1] // HEAD, 1, 2 * HEAD))
            knd, kng = per_head(kn[:, :2 * QK_A]), per_head(kn[:, 2 * QK_A:])
            od = _flash(lat[0], knd, cpr[1], cpr[2], lat[1], lat[2], 1, FLASH_COLS, F32)
            og = _flash(lat[3], kng, cpr[4], cpr[5], lat[4], lat[5], GQ_GROUP, FLASH_COLS // GQ_GROUP, BF16)
            x_new = _attn_out(od, og, *vecs, w_out, x, g1, lam_init, ATTN_OUT_TILE)
            if update_ctx:
                cod = _flash(cpr[0], knd, cpr[1], cpr[2], None, None, 1, FLASH_COLS, F32)
                cog = _flash(cpr[3], kng, cpr[4], cpr[5], None, None, GQ_GROUP, FLASH_COLS // GQ_GROUP, BF16)
                ctx = _attn_out(cod, cog, *vecs, w_out, ctx, cg1, lam_init, ATTN_OUT_TILE)
            x = x_new
        else:
            w_in = sgu_w_in[i].astype(BF16)
            w_s = sgu_w_s[i].astype(BF16)
            w_out = sgu_w_out[i].astype(BF16)
            vn = sgu_v_norm[i].reshape(1, -1)
            gw = sgu_w_out.shape[1] // SGU_GROUPS
            bs_full = jnp.repeat(sgu_b_s[i].T, gw, axis=1)
            x_new = _sgu(x, sh1, sc1, g1, nw1, w_in, vn, w_s, bs_full, w_out, SGU_TILE)
            if update_ctx:
                ctx = _sgu(ctx, csh1, csc1, cg1, nw1, w_in, vn, w_s, bs_full, w_out, SGU_TILE)
            x = x_new

        w_up = ffn_w_up[l].astype(BF16)
        w_dn = ffn_w_down[l].astype(BF16)
        cb = ffn_conv_b[l].reshape(1, -1)
        x = _ffn(x, sh2, sc2, g2, nw2, w_up, ffn_conv_w[l], cb, w_dn, fn, l == depth - 1, FFN_TILE)
        if update_ctx:
            ctx = _ffn(ctx, csh2, csc2, cg2, nw2, w_up, ffn_conv_w[l], cb, w_dn, fn, False, FFN_TILE)
    return x
```

```python
import functools
import math

import jax
import jax.numpy as jnp
from jax import lax
from jax.experimental import pallas as pl
from jax.experimental.pallas import tpu as pltpu

F32 = jnp.float32
BF16 = jnp.bfloat16

EPS = 1e-6
HEAD = 64
N_DIFF = 4
DIFF_V = 2 * HEAD
N_GQ = 8
N_GKV = 2
GQ_GROUP = N_GQ // N_GKV
ROPE_THETA = 10000.0
GRID_COLS = 64
N_MOD = 6
SGU_CHUNK = 128
SGU_GROUPS = 4
SUBLANES = 8
CONV_HALO = SUBLANES
V_EXT = 2 * SUBLANES
MOD_TILE_N = 1536
PROJ_TILE = 512
FLASH_COLS = 4096
ATTN_OUT_TILE = 1024
SGU_TILE = 512
FFN_TILE = 512
QK_SCALE = HEAD ** -0.5 * math.log2(math.e)
VMEM_LIMIT = 56 * 1024 * 1024
BOUND_SLACK = 1.0 + 2.0 ** -6
MIN_DENOM = 2.0 ** -80

QK_A = N_DIFF * HEAD
C_Q1, C_Q2, C_K1, C_K2 = 0, QK_A, 2 * QK_A, 3 * QK_A
C_VA = 4 * QK_A
C_QB = C_VA + N_DIFF * DIFF_V
C_KB = C_QB + N_GQ * HEAD
C_VB = C_KB + N_GKV * HEAD
C_END = C_VB + N_GKV * HEAD


def _cparams(n_axes):
    return pltpu.CompilerParams(dimension_semantics=("arbitrary",) * n_axes,
                                vmem_limit_bytes=VMEM_LIMIT)


def _const_spec(shape):
    nd = len(shape)
    return pl.BlockSpec(shape, lambda *_: (0,) * nd, pipeline_mode=pl.Buffered(1))


def _rms(xf, g):
    ms = jnp.mean(xf * xf, axis=-1, keepdims=True)
    return xf * lax.rsqrt(ms + EPS) * g


def _sigmoid(x):
    return 1.0 / (1.0 + jnp.exp(-x))


def _mod_kernel(c_ref, w_ref, b_ref, o_ref):
    c = c_ref[...]
    s = c * _sigmoid(c)
    o_ref[...] = jnp.dot(s, w_ref[...], precision=lax.Precision.HIGHEST,
                         preferred_element_type=F32) + b_ref[...]


def _mod_vectors(cvec, ada_w, ada_b):
    depth, d, n = ada_w.shape
    rows = cvec.shape[0]
    tn = MOD_TILE_N
    return pl.pallas_call(
        _mod_kernel,
        out_shape=jax.ShapeDtypeStruct((depth, rows, n), F32),
        grid=(depth, n // tn),
        in_specs=[pl.BlockSpec((rows, d), lambda l, j: (0, 0)),
                  pl.BlockSpec((None, d, tn), lambda l, j: (l, 0, j)),
                  pl.BlockSpec((None, 1, tn), lambda l, j: (l, 0, j))],
        out_specs=pl.BlockSpec((None, rows, tn), lambda l, j: (l, 0, j)),
        compiler_params=_cparams(2),
        name="mod_vectors",
    )(cvec, ada_w, ada_b.reshape(depth, 1, n))


def _group_sumsq(a, e):
    sq = a * a
    hi = sq.astype(BF16)
    lo = (sq - hi.astype(F32)).astype(BF16)
    return (jnp.dot(hi, e, preferred_element_type=F32) + jnp.dot(lo, e, preferred_element_type=F32))


def _pair_partner(a):
    n = a.shape[1]
    even = lax.broadcasted_iota(jnp.int32, (1, n), 1) % 2 == 0
    return jnp.where(even, -pltpu.roll(a, n - 1, axis=1), pltpu.roll(a, 1, axis=1))


def _attn_proj_kernel(x_ref, sh_ref, sc_ref, nw_ref, w_ref, cos_ref, sin_ref,
                      gq_ref, gqs_ref, gk_ref, gks_ref, e_ref,
                      qd_ref, kd_ref, va_ref, qg_ref, kg_ref, vg_ref, kn_ref):
    x = x_ref[...]
    h = (_rms(x, nw_ref[...]) * (1.0 + sc_ref[...]) + sh_ref[...]).astype(BF16)
    a = jnp.dot(h, w_ref[...], preferred_element_type=F32)
    cos = cos_ref[...]
    sin = sin_ref[...]
    scale = QK_SCALE

    for j in range(4):
        aj = a[:, C_Q1 + QK_A * j:C_Q1 + QK_A * (j + 1)]
        r = aj * cos + _pair_partner(aj) * sin
        if j < 2:
            rt = (r * scale).T.astype(BF16)
            for hh in range(N_DIFF):
                qd_ref[j * N_DIFF + hh] = rt[HEAD * hh:HEAD * (hh + 1), :]
        else:
            r = r.astype(BF16)
            for hh in range(N_DIFF):
                kd_ref[(j - 2) * N_DIFF + hh] = r[:, HEAD * hh:HEAD * (hh + 1)]
            ssq = _group_sumsq(r.astype(F32), e_ref[:QK_A, :QK_A])
            kn_ref[:, QK_A * (j - 2):QK_A * (j - 1)] = jnp.max(ssq, axis=0, keepdims=True)
    t = x.shape[0]
    ext = (lax.broadcasted_iota(jnp.int32, (V_EXT, t), 0) == 0).astype(BF16)
    for hh in range(N_DIFF):
        va_ref[hh, 0, :DIFF_V, :] = a[:, C_VA + DIFF_V * hh:C_VA + DIFF_V * (hh + 1)].T.astype(BF16)
        va_ref[hh, 0, DIFF_V:, :] = ext

    e = e_ref[...]
    cos2 = jnp.concatenate([cos, cos], axis=1)
    sin2 = jnp.concatenate([sin, sin], axis=1)
    qb = a[:, C_QB:C_KB]
    rq = lax.rsqrt(_group_sumsq(qb, e) * (1.0 / HEAD) + EPS)
    qt = (rq * (qb * (gq_ref[...] * cos2) + _pair_partner(qb) * (gqs_ref[...] * sin2)) * scale).T.astype(BF16)
    for hh in range(N_GQ):
        qg_ref[hh] = qt[HEAD * hh:HEAD * (hh + 1), :]
    nk = N_GKV * HEAD
    kb = a[:, C_KB:C_VB]
    rk = lax.rsqrt(_group_sumsq(kb, e[:nk, :nk]) * (1.0 / HEAD) + EPS)
    k = (rk * (kb * (gk_ref[...] * cos[:, :nk]) + _pair_partner(kb) * (gks_ref[...] * sin[:, :nk]))).astype(BF16)
    kn_ref[:, 2 * QK_A:] = jnp.max(_group_sumsq(k.astype(F32), e[:nk, :nk]), axis=0, keepdims=True)
    vbt = a[:, C_VB:C_END].T.astype(BF16)
    for hh in range(N_GKV):
        kg_ref[hh] = k[:, HEAD * hh:HEAD * (hh + 1)]
        vg_ref[hh, 0, :HEAD, :] = vbt[HEAD * hh:HEAD * (hh + 1), :]
        vg_ref[hh, 0, HEAD:, :] = ext


def _attn_proj(x, sh, sc, nw, w_qkv, cos, sin, gq, gqs, gk, gks, e, tile):
    b, s, d = x.shape
    t = min(tile, s)
    hd = lambda n, w: jax.ShapeDtypeStruct((b, n, s, w), BF16)
    hspec = lambda n, w: pl.BlockSpec((None, n, t, w), lambda bi, i: (bi, 0, i, 0))
    qd = lambda n: jax.ShapeDtypeStruct((b, n, HEAD, s), BF16)
    qspec = lambda n: pl.BlockSpec((None, n, HEAD, t), lambda bi, i: (bi, 0, 0, i))
    vd = lambda n, w: jax.ShapeDtypeStruct((b, n, s // t, w + V_EXT, t), BF16)
    vspec = lambda n, w: pl.BlockSpec((None, n, 1, w + V_EXT, t), lambda bi, i: (bi, 0, i, 0, 0))
    vec = pl.BlockSpec((None, 1, d), lambda bi, i: (bi, 0, 0))
    n_kn = (2 * N_DIFF + N_GKV) * HEAD
    return pl.pallas_call(
        _attn_proj_kernel,
        out_shape=(qd(2 * N_DIFF), hd(2 * N_DIFF, HEAD), vd(N_DIFF, DIFF_V),
                   qd(N_GQ), hd(N_GKV, HEAD), vd(N_GKV, HEAD),
                   jax.ShapeDtypeStruct((b, s // t, 1, n_kn), F32)),
        grid=(b, s // t),
        in_specs=[pl.BlockSpec((None, t, d), lambda bi, i: (bi, i, 0)), vec, vec,
                  _const_spec((1, d)), _const_spec(w_qkv.shape),
                  pl.BlockSpec((t, 4 * HEAD), lambda bi, i: (i, 0)),
                  pl.BlockSpec((t, 4 * HEAD), lambda bi, i: (i, 0)),
                  _const_spec(gq.shape), _const_spec(gqs.shape),
                  _const_spec(gk.shape), _const_spec(gks.shape), _const_spec(e.shape)],
        out_specs=(qspec(2 * N_DIFF), hspec(2 * N_DIFF, HEAD), vspec(N_DIFF, DIFF_V),
                   qspec(N_GQ), hspec(N_GKV, HEAD), vspec(N_GKV, HEAD),
                   pl.BlockSpec((None, None, 1, n_kn), lambda bi, i: (bi, i, 0, 0))),
        compiler_params=_cparams(2),
        name="attn_proj",
    )(x, sh, sc, nw, w_qkv, cos, sin, gq, gqs, gk, gks, e)


def _flash_kernel(*refs, group, tq, dv, n_lat, unroll):
    if n_lat:
        q_ref, kn_ref, kc_ref, vc_ref, kl_ref, vl_ref, o_ref, m_scr, acc_scr, st0, st1, mx0, mx1 = refs
    else:
        q_ref, kn_ref, kc_ref, vc_ref, o_ref, m_scr, acc_scr = refs
    cols = group * tq
    qt = q_ref[0] if group == 1 else jnp.concatenate([q_ref[g] for g in range(group)], axis=1)

    def finalize():
        acc = acc_scr[...]
        o = (acc[:dv] / acc[dv:dv + 1]).T
        if group > 1:
            o = jnp.concatenate([o[g * tq:(g + 1) * tq] for g in range(group)], axis=1)
        o_ref[...] = o.astype(o_ref.dtype)

    qf = qt.astype(F32)
    bound = jnp.sqrt(jnp.sum(qf * qf, axis=0, keepdims=True) * kn_ref[:, :1]) * BOUND_SLACK
    acc_scr[...] = jnp.zeros(acc_scr.shape, F32)

    def add(k, vt):
        st = jnp.dot(k, qt, preferred_element_type=F32)
        acc_scr[...] += jnp.dot(vt, jnp.exp2(st - bound).astype(BF16), preferred_element_type=F32)

    add(kc_ref[...], vc_ref[...])
    if n_lat:
        tk = vl_ref.shape[2]

        def fast_body(jj, carry):
            for u in range(unroll):
                j = unroll * jj + u
                start = pl.multiple_of(j * tk, tk)
                add(kl_ref[pl.ds(start, tk), :], vl_ref[j])
            return carry

        lax.fori_loop(0, n_lat // unroll, fast_body, 0)
    ok = jnp.min(acc_scr[dv:dv + 1, :]) > MIN_DENOM

    def scores(k):
        st = jnp.dot(k, qt, preferred_element_type=F32)
        return st, jnp.max(st, axis=0, keepdims=True)

    def consume(st, mx, vt):
        m_old = m_scr[...]
        m_new = jnp.maximum(m_old, mx)
        alpha = jnp.exp2(m_old - m_new)
        pt = jnp.exp2(st - m_new).astype(BF16)
        acc_scr[...] = alpha * acc_scr[...] + jnp.dot(vt, pt, preferred_element_type=F32)
        m_scr[...] = m_new

    def exact():
        m_scr[...] = jnp.full(m_scr.shape, -jnp.inf, F32)
        acc_scr[...] = jnp.zeros(acc_scr.shape, F32)
        consume(*scores(kc_ref[...]), vc_ref[...])
        if n_lat:
            def produce(j, st_ref, mx_ref):
                start = pl.multiple_of(j * tk, tk)
                st_ref[...], mx_ref[...] = scores(kl_ref[pl.ds(start, tk), :])

            bufs = ((st0, mx0), (st1, mx1))

            def run(j, produce_next):
                for u in range(len(bufs)):
                    if u < len(bufs) - 1 or produce_next:
                        produce(j + u + 1, *bufs[(u + 1) % 2])
                    st_ref, mx_ref = bufs[u % 2]
                    consume(st_ref[...], mx_ref[...], vl_ref[j + u])

            produce(0, st0, mx0)

            def body(jj, carry):
                run(len(bufs) * jj, True)
                return carry

            lax.fori_loop(0, n_lat // len(bufs) - 1, body, 0)
            run(n_lat - len(bufs), False)

    @pl.when(jnp.logical_not(ok))
    def _():
        exact()

    finalize()


def _flash(q, kn, kc, vc, kl, vl, group, tq, out_dtype):
    b, hq, _, s = q.shape
    hk, sc = kc.shape[1], kc.shape[2]
    hv, dvx = vc.shape[1], vc.shape[3]
    dv = dvx - V_EXT
    tq = min(tq, s)
    cols = group * tq
    n_lat = 0 if kl is None else vl.shape[2]
    in_specs = [pl.BlockSpec((None, group, HEAD, tq), lambda bi, h, i: (bi, h, 0, i)),
                pl.BlockSpec((None, None, 1, 2 * HEAD), lambda bi, h, i: (bi, h, 0, 0)),
                pl.BlockSpec((None, None, sc, HEAD), lambda bi, h, i: (bi, h, 0, 0)),
                pl.BlockSpec((None, None, None, dvx, sc), lambda bi, h, i: (bi, h % hv, 0, 0, 0))]
    args = [q, kn, kc, vc]
    scratch = [pltpu.VMEM((1, cols), F32), pltpu.VMEM((dvx, cols), F32)]
    if n_lat:
        assert n_lat % 2 == 0, "latent key chunks alternate between two score buffers"
        sl, tk = kl.shape[2], vl.shape[4]
        in_specs += [pl.BlockSpec((None, None, sl, HEAD), lambda bi, h, i: (bi, h, 0, 0)),
                     pl.BlockSpec((None, None, n_lat, dvx, tk), lambda bi, h, i: (bi, h % hv, 0, 0, 0))]
        args += [kl, vl]
        scratch += [pltpu.VMEM((tk, cols), F32), pltpu.VMEM((tk, cols), F32),
                    pltpu.VMEM((1, cols), F32), pltpu.VMEM((1, cols), F32)]
    unroll = next(u for u in (8, 4, 2) if n_lat % u == 0)
    kern = functools.partial(_flash_kernel, group=group, tq=tq, dv=dv, n_lat=n_lat, unroll=unroll)
    return pl.pallas_call(
        kern,
        out_shape=jax.ShapeDtypeStruct((b, s, hq * dv), out_dtype),
        grid=(b, hk, s // tq),
        in_specs=in_specs,
        out_specs=pl.BlockSpec((None, tq, group * dv), lambda bi, h, i: (bi, i, h)),
        scratch_shapes=scratch,
        compiler_params=_cparams(3),
        name="flash_g%d_l%d" % (group, n_lat),
    )(*args)


def _attn_out_kernel(od_ref, og_ref, lq1_ref, lk1_ref, lq2_ref, lk2_ref, subln_ref, w_ref,
                     x_ref, g1_ref, o_ref, *, lam_init):
    lam = (jnp.exp(jnp.sum(lq1_ref[...] * lk1_ref[...], axis=-1, keepdims=True))
           - jnp.exp(jnp.sum(lq2_ref[...] * lk2_ref[...], axis=-1, keepdims=True)) + lam_init)
    od = od_ref[...]
    half = N_DIFF * DIFF_V
    parts = []
    for hh in range(N_DIFF):
        dh = od[:, DIFF_V * hh:DIFF_V * (hh + 1)] - lam * od[:, half + DIFF_V * hh:half + DIFF_V * (hh + 1)]
        parts.append(_rms(dh, subln_ref[...]) * (1.0 - lam_init))
    att = jnp.concatenate([p.astype(BF16) for p in parts] + [og_ref[...]], axis=1)
    y = jnp.dot(att, w_ref[...], preferred_element_type=F32)
    o_ref[...] = x_ref[...] + g1_ref[...] * y


def _attn_out(od, og, lq1, lk1, lq2, lk2, subln, w_out, x, g1, lam_init, tile):
    b, s, d = x.shape
    t = min(tile, s)
    vec = pl.BlockSpec((None, 1, d), lambda bi, i: (bi, 0, 0))
    row = lambda a: pl.BlockSpec((None, t, a.shape[2]), lambda bi, i: (bi, i, 0))
    return pl.pallas_call(
        functools.partial(_attn_out_kernel, lam_init=lam_init),
        out_shape=jax.ShapeDtypeStruct(x.shape, F32),
        grid=(b, s // t),
        in_specs=[row(od), row(og), _const_spec(lq1.shape), _const_spec(lk1.shape),
                  _const_spec(lq2.shape), _const_spec(lk2.shape), _const_spec(subln.shape),
                  _const_spec(w_out.shape), row(x), vec],
        out_specs=row(x),
        compiler_params=_cparams(2),
        name="attn_out",
    )(od, og, lq1, lk1, lq2, lk2, subln, w_out, x, g1)


def _sgu_kernel(x_ref, sh_ref, sc_ref, g1_ref, nw_ref, win_ref, vn_ref, ws_ref, bs_ref, wout_ref, o_ref):
    x = x_ref[...]
    t, d = x.shape
    h = (_rms(x, nw_ref[...]) * (1.0 + sc_ref[...]) + sh_ref[...]).astype(BF16)
    z = jnp.dot(h, win_ref[...], preferred_element_type=F32)
    z = 0.5 * z * (1.0 + lax.erf(z * (2.0 ** -0.5)))
    sd = z.shape[1] // 2
    u = z[:, :sd]
    v = _rms(z[:, sd:], vn_ref[...]).astype(BF16)
    gw = sd // SGU_GROUPS
    rows = []
    for n in range(t // SGU_CHUNK):
        cols = []
        for g in range(SGU_GROUPS):
            vg = v[SGU_CHUNK * n:SGU_CHUNK * (n + 1), gw * g:gw * (g + 1)]
            cols.append(jnp.dot(ws_ref[g], vg, preferred_element_type=F32))
        rows.append(jnp.concatenate(cols, axis=1) + bs_ref[...])
    mixed = jnp.concatenate(rows, axis=0) if len(rows) > 1 else rows[0]
    y = jnp.dot((u * mixed).astype(BF16), wout_ref[...], preferred_element_type=F32)
    o_ref[...] = x + g1_ref[...] * y


def _sgu(x, sh, sc, g1, nw, w_in, vn, w_s, bs_full, w_out, tile):
    b, s, d = x.shape
    t = min(tile, s)
    vec = pl.BlockSpec((None, 1, d), lambda bi, i: (bi, 0, 0))
    row = pl.BlockSpec((None, t, d), lambda bi, i: (bi, i, 0))
    return pl.pallas_call(
        _sgu_kernel,
        out_shape=jax.ShapeDtypeStruct(x.shape, F32),
        grid=(b, s // t),
        in_specs=[row, vec, vec, vec, _const_spec(nw.shape), _const_spec(w_in.shape),
                  _const_spec(vn.shape), _const_spec(w_s.shape), _const_spec(bs_full.shape),
                  _const_spec(w_out.shape)],
        out_specs=row,
        compiler_params=_cparams(2),
        name="sgu",
    )(x, sh, sc, g1, nw, w_in, vn, w_s, bs_full, w_out)


def _ffn_kernel(xp_ref, x_ref, xn_ref, sh_ref, sc_ref, g2_ref, nw_ref, wup_ref, cw_ref, cb_ref,
                wdn_ref, fn_ref, o_ref, z_scr, *, final_norm):
    i = pl.program_id(1)
    last = pl.num_programs(1) - 1
    x = x_ref[...]
    t = x.shape[0]
    xa = jnp.concatenate([xp_ref[...], x, xn_ref[...]], axis=0)
    h = _rms(xa, nw_ref[...]) * (1.0 + sc_ref[...]) + sh_ref[...]
    r = lax.broadcasted_iota(jnp.int32, (t + 2 * CONV_HALO, 1), 0)
    outside = ((r < CONV_HALO) & (i == 0)) | ((r >= t + CONV_HALO) & (i == last))
    h = jnp.where(outside, 0.0, h).astype(BF16)
    z_scr[...] = jnp.dot(h, wup_ref[...], preferred_element_type=F32)
    cw = cw_ref[...]
    zc = (cw[0:1] * z_scr[pl.ds(CONV_HALO - 1, t), :] + cw[1:2] * z_scr[pl.ds(CONV_HALO, t), :]
          + cw[2:3] * z_scr[pl.ds(CONV_HALO + 1, t), :] + cb_ref[...])
    f = zc.shape[1] // 2
    g = zc[:, :f]
    act = (g * _sigmoid(g) * zc[:, f:]).astype(BF16)
    y = jnp.dot(act, wdn_ref[...], preferred_element_type=F32)
    out = x + g2_ref[...] * y
    if final_norm:
        out = _rms(out, fn_ref[...])
    o_ref[...] = out


def _ffn(x, sh, sc, g2, nw, w_up, conv_w, conv_b, w_dn, fn, final_norm, tile):
    b, s, d = x.shape
    t = min(tile, s)
    tb = t // CONV_HALO
    nhb = s // CONV_HALO
    vec = pl.BlockSpec((None, 1, d), lambda bi, i: (bi, 0, 0))
    row = pl.BlockSpec((None, t, d), lambda bi, i: (bi, i, 0))
    prev = pl.BlockSpec((None, CONV_HALO, d), lambda bi, i: (bi, jnp.maximum(i * tb - 1, 0), 0))
    nxt = pl.BlockSpec((None, CONV_HALO, d), lambda bi, i: (bi, jnp.minimum((i + 1) * tb, nhb - 1), 0))
    return pl.pallas_call(
        functools.partial(_ffn_kernel, final_norm=final_norm),
        out_shape=jax.ShapeDtypeStruct(x.shape, F32),
        grid=(b, s // t),
        in_specs=[prev, row, nxt, vec, vec, vec, _const_spec(nw.shape), _const_spec(w_up.shape),
                  _const_spec(conv_w.shape), _const_spec(conv_b.shape), _const_spec(w_dn.shape),
                  _const_spec(fn.shape)],
        out_specs=row,
        scratch_shapes=[pltpu.VMEM((t + 2 * CONV_HALO, w_up.shape[1]), F32)],
        compiler_params=_cparams(2),
        name="ffn",
    )(x, x, x, sh, sc, g2, nw, w_up, conv_w, conv_b, w_dn, fn)


def _pair_swap_vec(g):
    return g.reshape(-1, 2)[:, ::-1].reshape(-1)


def _rope_tables(rows):
    n_freq = HEAD // 4
    inv = ROPE_THETA ** (-jnp.arange(n_freq, dtype=F32) / n_freq)
    row = jnp.repeat(jnp.arange(rows, dtype=F32), GRID_COLS)
    col = jnp.tile(jnp.arange(GRID_COLS, dtype=F32), rows)
    ang = jnp.concatenate([row[:, None] * inv, col[:, None] * inv], axis=-1)
    expand = lambda t: jnp.tile(jnp.repeat(t, 2, axis=-1), (1, N_DIFF))
    return expand(jnp.cos(ang)), expand(jnp.sin(ang))


def kernel(x, c, ctx, c_ctx, ada_w, ada_b, mix_norm, ffn_norm, final_norm, attn_w_in, attn_w_out,
           diff_lq1, diff_lk1, diff_lq2, diff_lk2, diff_subln, gqa_q_norm, gqa_k_norm, sgu_w_in,
           sgu_v_norm, sgu_w_s, sgu_b_s, sgu_w_out, ffn_w_up, ffn_conv_w, ffn_conv_b, ffn_w_down):
    b, s, d = x.shape
    n_ctx = ctx.shape[1]
    depth = ada_w.shape[0]
    last_attn = (depth - 1) // 2 * 2

    rows_pad = -(-(b + 1) // SUBLANES) * SUBLANES
    cvec = jnp.zeros((rows_pad, d), F32).at[:b].set(c).at[b].set(c_ctx)
    mods = _mod_vectors(cvec, ada_w, ada_b)

    cos_lat, sin_lat = _rope_tables(s // GRID_COLS)
    cos_ctx = jnp.ones((n_ctx, 4 * HEAD), F32)
    sin_ctx = jnp.zeros((n_ctx, 4 * HEAD), F32)
    eye = jnp.repeat(jnp.repeat(jnp.eye(N_GQ, dtype=BF16), HEAD, axis=0), HEAD, axis=1)
    fn = final_norm.reshape(1, d)

    def split_mod(m):
        return [m[:, k * d:(k + 1) * d].reshape(b, 1, d) for k in range(N_MOD)]

    for l in range(depth):
        i = l // 2
        is_attn = l % 2 == 0
        update_ctx = l < last_attn
        sh1, sc1, g1, sh2, sc2, g2 = split_mod(mods[l, :b])
        need_ctx = is_attn or update_ctx
        if need_ctx:
            csh1, csc1, cg1, csh2, csc2, cg2 = split_mod(jnp.broadcast_to(mods[l, b], (b, N_MOD * d)))
        nw1 = mix_norm[l].reshape(1, d)
        nw2 = ffn_norm[l].reshape(1, d)

        if is_attn:
            lam_init = 0.8 - 0.6 * math.exp(-0.3 * l)
            w = attn_w_in[i]
            w_qkv = w.astype(BF16)
            gq = jnp.tile(gqa_q_norm[i], N_GQ).reshape(1, -1)
            gqs = jnp.tile(_pair_swap_vec(gqa_q_norm[i]), N_GQ).reshape(1, -1)
            gk = jnp.tile(gqa_k_norm[i], N_GKV).reshape(1, -1)
            gks = jnp.tile(_pair_swap_vec(gqa_k_norm[i]), N_GKV).reshape(1, -1)
            w_out = attn_w_out[i].astype(BF16)
            vecs = [v[i].reshape(1, -1) for v in (diff_lq1, diff_lk1, diff_lq2, diff_lk2, diff_subln)]

            lat = _attn_proj(x, sh1, sc1, nw1, w_qkv, cos_lat, sin_lat, gq, gqs, gk, gks, eye, PROJ_TILE)
            cpr = _attn_proj(ctx, csh1, csc1, nw1, w_qkv, cos_ctx, sin_ctx, gq, gqs, gk, gks, eye, n_ctx)
            kn = jnp.maximum(jnp.max(lat[6], axis=1), jnp.max(cpr[6], axis=1))[:, 0]
            per_head = lambda v: jnp.broadcast_to(v.reshape(b, -1, HEAD)[:, :, :1, None],
                                                  (b, v.shape[1] // HEAD, 1, 2 * HEAD))
            knd, kng = per_head(kn[:, :2 * QK_A]), per_head(kn[:, 2 * QK_A:])
            od = _flash(lat[0], knd, cpr[1], cpr[2], lat[1], lat[2], 1, FLASH_COLS, F32)
            og = _flash(lat[3], kng, cpr[4], cpr[5], lat[4], lat[5], GQ_GROUP, FLASH_COLS // GQ_GROUP, BF16)
            x_new = _attn_out(od, og, *vecs, w_out, x, g1, lam_init, ATTN_OUT_TILE)
            if update_ctx:
                cod = _flash(cpr[0], knd, cpr[1], cpr[2], None, None, 1, FLASH_COLS, F32)
                cog = _flash(cpr[3], kng, cpr[4], cpr[5], None, None, GQ_GROUP, FLASH_COLS // GQ_GROUP, BF16)
                ctx = _attn_out(cod, cog, *vecs, w_out, ctx, cg1, lam_init, ATTN_OUT_TILE)
            x = x_new
        else:
            w_in = sgu_w_in[i].astype(BF16)
            w_s = sgu_w_s[i].astype(BF16)
            w_out = sgu_w_out[i].astype(BF16)
            vn = sgu_v_norm[i].reshape(1, -1)
            gw = sgu_w_out.shape[1] // SGU_GROUPS
            bs_full = jnp.repeat(sgu_b_s[i].T, gw, axis=1)
            x_new = _sgu(x, sh1, sc1, g1, nw1, w_in, vn, w_s, bs_full, w_out, SGU_TILE)
            if update_ctx:
                ctx = _sgu(ctx, csh1, csc1, cg1, nw1, w_in, vn, w_s, bs_full, w_out, SGU_TILE)
            x = x_new

        w_up = ffn_w_up[l].astype(BF16)
        w_dn = ffn_w_down[l].astype(BF16)
        cb = ffn_conv_b[l].reshape(1, -1)
        x = _ffn(x, sh2, sc2, g2, nw2, w_up, ffn_conv_w[l], cb, w_dn, fn, l == depth - 1, FFN_TILE)
        if update_ctx:
            ctx = _ffn(ctx, csh2, csc2, cg2, nw2, w_up, ffn_conv_w[l], cb, w_dn, fn, False, FFN_TILE)
    return x
```

```python
import functools
import math

import jax
import jax.numpy as jnp
from jax import lax
from jax.experimental import pallas as pl
from jax.experimental.pallas import tpu as pltpu

F32 = jnp.float32
BF16 = jnp.bfloat16

EPS = 1e-6
HEAD = 64
N_DIFF = 4
DIFF_V = 2 * HEAD
N_GQ = 8
N_GKV = 2
GQ_GROUP = N_GQ // N_GKV
ROPE_THETA = 10000.0
GRID_COLS = 64
N_MOD = 6
SGU_CHUNK = 128
SGU_GROUPS = 4
SUBLANES = 8
CONV_HALO = SUBLANES
STREAM_ROWS_PER_PUSH = 128
V_EXT = 2 * SUBLANES
MOD_TILE_N = 1536
PROJ_TILE = 512
FLASH_COLS = 4096
ATTN_OUT_TILE = 1024
SGU_TILE = 512
FFN_TILE = 512
QK_SCALE = HEAD ** -0.5 * math.log2(math.e)
VMEM_LIMIT = 56 * 1024 * 1024
BOUND_SLACK = 1.0 + 2.0 ** -6
MIN_DENOM = 2.0 ** -80

QK_A = N_DIFF * HEAD
C_Q1, C_Q2, C_K1, C_K2 = 0, QK_A, 2 * QK_A, 3 * QK_A
C_VA = 4 * QK_A
C_QB = C_VA + N_DIFF * DIFF_V
C_KB = C_QB + N_GQ * HEAD
C_VB = C_KB + N_GKV * HEAD
C_END = C_VB + N_GKV * HEAD


def _cparams(n_axes):
    return pltpu.CompilerParams(dimension_semantics=("arbitrary",) * n_axes,
                                vmem_limit_bytes=VMEM_LIMIT)


def _const_spec(shape):
    nd = len(shape)
    return pl.BlockSpec(shape, lambda *_: (0,) * nd, pipeline_mode=pl.Buffered(1))


def _rms(xf, g):
    ms = jnp.mean(xf * xf, axis=-1, keepdims=True)
    return xf * lax.rsqrt(ms + EPS) * g


def _sigmoid(x):
    return 1.0 / (1.0 + jnp.exp(-x))


def _mod_kernel(c_ref, w_ref, b_ref, o_ref):
    c = c_ref[...]
    s = c * _sigmoid(c)
    o_ref[...] = jnp.dot(s, w_ref[...], precision=lax.Precision.HIGHEST,
                         preferred_element_type=F32) + b_ref[...]


def _mod_vectors(cvec, ada_w, ada_b):
    depth, d, n = ada_w.shape
    rows = cvec.shape[0]
    tn = MOD_TILE_N
    return pl.pallas_call(
        _mod_kernel,
        out_shape=jax.ShapeDtypeStruct((depth, rows, n), F32),
        grid=(depth, n // tn),
        in_specs=[pl.BlockSpec((rows, d), lambda l, j: (0, 0)),
                  pl.BlockSpec((None, d, tn), lambda l, j: (l, 0, j)),
                  pl.BlockSpec((None, 1, tn), lambda l, j: (l, 0, j))],
        out_specs=pl.BlockSpec((None, rows, tn), lambda l, j: (l, 0, j)),
        compiler_params=_cparams(2),
        name="mod_vectors",
    )(cvec, ada_w, ada_b.reshape(depth, 1, n))


def _group_sumsq(a, e):
    sq = a * a
    hi = sq.astype(BF16)
    lo = (sq - hi.astype(F32)).astype(BF16)
    return (jnp.dot(hi, e, preferred_element_type=F32) + jnp.dot(lo, e, preferred_element_type=F32))


def _pair_partner(a):
    n = a.shape[1]
    even = lax.broadcasted_iota(jnp.int32, (1, n), 1) % 2 == 0
    return jnp.where(even, -pltpu.roll(a, n - 1, axis=1), pltpu.roll(a, 1, axis=1))


def _attn_proj_kernel(x_ref, sh_ref, sc_ref, nw_ref, w_ref, cos_ref, sin_ref,
                      gq_ref, gqs_ref, gk_ref, gks_ref, e_ref,
                      qd_ref, kd_ref, va_ref, qg_ref, kg_ref, vg_ref, kn_ref):
    x = x_ref[...]
    h = (_rms(x, nw_ref[...]) * (1.0 + sc_ref[...]) + sh_ref[...]).astype(BF16)
    a = jnp.dot(h, w_ref[...], preferred_element_type=F32)
    cos = cos_ref[...]
    sin = sin_ref[...]
    scale = QK_SCALE

    for j in range(4):
        aj = a[:, C_Q1 + QK_A * j:C_Q1 + QK_A * (j + 1)]
        r = aj * cos + _pair_partner(aj) * sin
        if j < 2:
            rt = (r * scale).T.astype(BF16)
            for hh in range(N_DIFF):
                qd_ref[j * N_DIFF + hh] = rt[HEAD * hh:HEAD * (hh + 1), :]
        else:
            r = r.astype(BF16)
            for hh in range(N_DIFF):
                kd_ref[(j - 2) * N_DIFF + hh] = r[:, HEAD * hh:HEAD * (hh + 1)]
            ssq = _group_sumsq(r.astype(F32), e_ref[:QK_A, :QK_A])
            kn_ref[:, QK_A * (j - 2):QK_A * (j - 1)] = jnp.max(ssq, axis=0, keepdims=True)
    t = x.shape[0]
    ext = (lax.broadcasted_iota(jnp.int32, (V_EXT, t), 0) == 0).astype(BF16)
    for hh in range(N_DIFF):
        va_ref[hh, 0, :DIFF_V, :] = a[:, C_VA + DIFF_V * hh:C_VA + DIFF_V * (hh + 1)].T.astype(BF16)
        va_ref[hh, 0, DIFF_V:, :] = ext

    e = e_ref[...]
    cos2 = jnp.concatenate([cos, cos], axis=1)
    sin2 = jnp.concatenate([sin, sin], axis=1)
    qb = a[:, C_QB:C_KB]
    rq = lax.rsqrt(_group_sumsq(qb, e) * (1.0 / HEAD) + EPS)
    qt = (rq * (qb * (gq_ref[...] * cos2) + _pair_partner(qb) * (gqs_ref[...] * sin2)) * scale).T.astype(BF16)
    for hh in range(N_GQ):
        qg_ref[hh] = qt[HEAD * hh:HEAD * (hh + 1), :]
    nk = N_GKV * HEAD
    kb = a[:, C_KB:C_VB]
    rk = lax.rsqrt(_group_sumsq(kb, e[:nk, :nk]) * (1.0 / HEAD) + EPS)
    k = (rk * (kb * (gk_ref[...] * cos[:, :nk]) + _pair_partner(kb) * (gks_ref[...] * sin[:, :nk]))).astype(BF16)
    kn_ref[:, 2 * QK_A:] = jnp.max(_group_sumsq(k.astype(F32), e[:nk, :nk]), axis=0, keepdims=True)
    vbt = a[:, C_VB:C_END].T.astype(BF16)
    for hh in range(N_GKV):
        kg_ref[hh] = k[:, HEAD * hh:HEAD * (hh + 1)]
        vg_ref[hh, 0, :HEAD, :] = vbt[HEAD * hh:HEAD * (hh + 1), :]
        vg_ref[hh, 0, HEAD:, :] = ext


def _attn_proj(x, sh, sc, nw, w_qkv, cos, sin, gq, gqs, gk, gks, e, tile):
    b, s, d = x.shape
    t = min(tile, s)
    hd = lambda n, w: jax.ShapeDtypeStruct((b, n, s, w), BF16)
    hspec = lambda n, w: pl.BlockSpec((None, n, t, w), lambda bi, i: (bi, 0, i, 0))
    qd = lambda n: jax.ShapeDtypeStruct((b, n, HEAD, s), BF16)
    qspec = lambda n: pl.BlockSpec((None, n, HEAD, t), lambda bi, i: (bi, 0, 0, i))
    vd = lambda n, w: jax.ShapeDtypeStruct((b, n, s // t, w + V_EXT, t), BF16)
    vspec = lambda n, w: pl.BlockSpec((None, n, 1, w + V_EXT, t), lambda bi, i: (bi, 0, i, 0, 0))
    vec = pl.BlockSpec((None, 1, d), lambda bi, i: (bi, 0, 0))
    n_kn = (2 * N_DIFF + N_GKV) * HEAD
    return pl.pallas_call(
        _attn_proj_kernel,
        out_shape=(qd(2 * N_DIFF), hd(2 * N_DIFF, HEAD), vd(N_DIFF, DIFF_V),
                   qd(N_GQ), hd(N_GKV, HEAD), vd(N_GKV, HEAD),
                   jax.ShapeDtypeStruct((b, s // t, 1, n_kn), F32)),
        grid=(b, s // t),
        in_specs=[pl.BlockSpec((None, t, d), lambda bi, i: (bi, i, 0)), vec, vec,
                  _const_spec((1, d)), _const_spec(w_qkv.shape),
                  pl.BlockSpec((t, 4 * HEAD), lambda bi, i: (i, 0)),
                  pl.BlockSpec((t, 4 * HEAD), lambda bi, i: (i, 0)),
                  _const_spec(gq.shape), _const_spec(gqs.shape),
                  _const_spec(gk.shape), _const_spec(gks.shape), _const_spec(e.shape)],
        out_specs=(qspec(2 * N_DIFF), hspec(2 * N_DIFF, HEAD), vspec(N_DIFF, DIFF_V),
                   qspec(N_GQ), hspec(N_GKV, HEAD), vspec(N_GKV, HEAD),
                   pl.BlockSpec((None, None, 1, n_kn), lambda bi, i: (bi, i, 0, 0))),
        compiler_params=_cparams(2),
        name="attn_proj",
    )(x, sh, sc, nw, w_qkv, cos, sin, gq, gqs, gk, gks, e)


def _flash_kernel(*refs, group, tq, dv, n_lat, unroll):
    if n_lat:
        q_ref, kn_ref, kc_ref, vc_ref, kl_ref, vl_ref, o_ref, m_scr, acc_scr, st0, st1, mx0, mx1 = refs
    else:
        q_ref, kn_ref, kc_ref, vc_ref, o_ref, m_scr, acc_scr = refs
    cols = group * tq
    qt = q_ref[0] if group == 1 else jnp.concatenate([q_ref[g] for g in range(group)], axis=1)

    def finalize():
        acc = acc_scr[...]
        o = (acc[:dv] / acc[dv:dv + 1]).T
        if group > 1:
            o = jnp.concatenate([o[g * tq:(g + 1) * tq] for g in range(group)], axis=1)
        o_ref[...] = o.astype(o_ref.dtype)

    qf = qt.astype(F32)
    bound = jnp.sqrt(jnp.sum(qf * qf, axis=0, keepdims=True) * kn_ref[:, :1]) * BOUND_SLACK
    acc_scr[...] = jnp.zeros(acc_scr.shape, F32)
    vpu_denominator = dv + V_EXT > STREAM_ROWS_PER_PUSH
    if vpu_denominator:
        m_scr[...] = jnp.zeros(m_scr.shape, F32)

    def add(k, vt):
        st = jnp.dot(k, qt, preferred_element_type=F32)
        p = jnp.exp2(st - bound)
        if vpu_denominator:
            m_scr[...] += jnp.sum(p, axis=0, keepdims=True)
            acc_scr[:dv, :] += jnp.dot(vt[:dv], p.astype(BF16), preferred_element_type=F32)
        else:
            acc_scr[...] += jnp.dot(vt, p.astype(BF16), preferred_element_type=F32)

    add(kc_ref[...], vc_ref[...])
    if n_lat:
        tk = vl_ref.shape[2]

        def fast_body(jj, carry):
            for u in range(unroll):
                j = unroll * jj + u
                start = pl.multiple_of(j * tk, tk)
                add(kl_ref[pl.ds(start, tk), :], vl_ref[j])
            return carry

        lax.fori_loop(0, n_lat // unroll, fast_body, 0)
    if vpu_denominator:
        acc_scr[dv:dv + 1, :] = m_scr[...]
    ok = jnp.min(acc_scr[dv:dv + 1, :]) > MIN_DENOM

    def scores(k):
        st = jnp.dot(k, qt, preferred_element_type=F32)
        return st, jnp.max(st, axis=0, keepdims=True)

    def consume(st, mx, vt):
        m_old = m_scr[...]
        m_new = jnp.maximum(m_old, mx)
        alpha = jnp.exp2(m_old - m_new)
        pt = jnp.exp2(st - m_new).astype(BF16)
        acc_scr[...] = alpha * acc_scr[...] + jnp.dot(vt, pt, preferred_element_type=F32)
        m_scr[...] = m_new

    def exact():
        m_scr[...] = jnp.full(m_scr.shape, -jnp.inf, F32)
        acc_scr[...] = jnp.zeros(acc_scr.shape, F32)
        consume(*scores(kc_ref[...]), vc_ref[...])
        if n_lat:
            def produce(j, st_ref, mx_ref):
                start = pl.multiple_of(j * tk, tk)
                st_ref[...], mx_ref[...] = scores(kl_ref[pl.ds(start, tk), :])

            bufs = ((st0, mx0), (st1, mx1))

            def run(j, produce_next):
                for u in range(len(bufs)):
                    if u < len(bufs) - 1 or produce_next:
                        produce(j + u + 1, *bufs[(u + 1) % 2])
                    st_ref, mx_ref = bufs[u % 2]
                    consume(st_ref[...], mx_ref[...], vl_ref[j + u])

            produce(0, st0, mx0)

            def body(jj, carry):
                run(len(bufs) * jj, True)
                return carry

            lax.fori_loop(0, n_lat // len(bufs) - 1, body, 0)
            run(n_lat - len(bufs), False)

    @pl.when(jnp.logical_not(ok))
    def _():
        exact()

    finalize()


def _flash(q, kn, kc, vc, kl, vl, group, tq, out_dtype):
    b, hq, _, s = q.shape
    hk, sc = kc.shape[1], kc.shape[2]
    hv, dvx = vc.shape[1], vc.shape[3]
    dv = dvx - V_EXT
    tq = min(tq, s)
    cols = group * tq
    n_lat = 0 if kl is None else vl.shape[2]
    in_specs = [pl.BlockSpec((None, group, HEAD, tq), lambda bi, h, i: (bi, h, 0, i)),
                pl.BlockSpec((None, None, 1, 2 * HEAD), lambda bi, h, i: (bi, h, 0, 0)),
                pl.BlockSpec((None, None, sc, HEAD), lambda bi, h, i: (bi, h, 0, 0)),
                pl.BlockSpec((None, None, None, dvx, sc), lambda bi, h, i: (bi, h % hv, 0, 0, 0))]
    args = [q, kn, kc, vc]
    scratch = [pltpu.VMEM((1, cols), F32), pltpu.VMEM((dvx, cols), F32)]
    if n_lat:
        assert n_lat % 2 == 0, "latent key chunks alternate between two score buffers"
        sl, tk = kl.shape[2], vl.shape[4]
        in_specs += [pl.BlockSpec((None, None, sl, HEAD), lambda bi, h, i: (bi, h, 0, 0)),
                     pl.BlockSpec((None, None, n_lat, dvx, tk), lambda bi, h, i: (bi, h % hv, 0, 0, 0))]
        args += [kl, vl]
        scratch += [pltpu.VMEM((tk, cols), F32), pltpu.VMEM((tk, cols), F32),
                    pltpu.VMEM((1, cols), F32), pltpu.VMEM((1, cols), F32)]
    unroll = next(u for u in (8, 4, 2) if n_lat % u == 0)
    kern = functools.partial(_flash_kernel, group=group, tq=tq, dv=dv, n_lat=n_lat, unroll=unroll)
    return pl.pallas_call(
        kern,
        out_shape=jax.ShapeDtypeStruct((b, s, hq * dv), out_dtype),
        grid=(b, hk, s // tq),
        in_specs=in_specs,
        out_specs=pl.BlockSpec((None, tq, group * dv), lambda bi, h, i: (bi, i, h)),
        scratch_shapes=scratch,
        compiler_params=_cparams(3),
        name="flash_g%d_l%d" % (group, n_lat),
    )(*args)


def _attn_out_kernel(od_ref, og_ref, lq1_ref, lk1_ref, lq2_ref, lk2_ref, subln_ref, w_ref,
                     x_ref, g1_ref, o_ref, *, lam_init):
    lam = (jnp.exp(jnp.sum(lq1_ref[...] * lk1_ref[...], axis=-1, keepdims=True))
           - jnp.exp(jnp.sum(lq2_ref[...] * lk2_ref[...], axis=-1, keepdims=True)) + lam_init)
    od = od_ref[...]
    half = N_DIFF * DIFF_V
    parts = []
    for hh in range(N_DIFF):
        dh = od[:, DIFF_V * hh:DIFF_V * (hh + 1)] - lam * od[:, half + DIFF_V * hh:half + DIFF_V * (hh + 1)]
        parts.append(_rms(dh, subln_ref[...]) * (1.0 - lam_init))
    att = jnp.concatenate([p.astype(BF16) for p in parts] + [og_ref[...]], axis=1)
    y = jnp.dot(att, w_ref[...], preferred_element_type=F32)
    o_ref[...] = x_ref[...] + g1_ref[...] * y


def _attn_out(od, og, lq1, lk1, lq2, lk2, subln, w_out, x, g1, lam_init, tile):
    b, s, d = x.shape
    t = min(tile, s)
    vec = pl.BlockSpec((None, 1, d), lambda bi, i: (bi, 0, 0))
    row = lambda a: pl.BlockSpec((None, t, a.shape[2]), lambda bi, i: (bi, i, 0))
    return pl.pallas_call(
        functools.partial(_attn_out_kernel, lam_init=lam_init),
        out_shape=jax.ShapeDtypeStruct(x.shape, F32),
        grid=(b, s // t),
        in_specs=[row(od), row(og), _const_spec(lq1.shape), _const_spec(lk1.shape),
                  _const_spec(lq2.shape), _const_spec(lk2.shape), _const_spec(subln.shape),
                  _const_spec(w_out.shape), row(x), vec],
        out_specs=row(x),
        compiler_params=_cparams(2),
        name="attn_out",
    )(od, og, lq1, lk1, lq2, lk2, subln, w_out, x, g1)


def _sgu_kernel(x_ref, sh_ref, sc_ref, g1_ref, nw_ref, win_ref, vn_ref, ws_ref, bs_ref, wout_ref, o_ref):
    x = x_ref[...]
    t, d = x.shape
    h = (_rms(x, nw_ref[...]) * (1.0 + sc_ref[...]) + sh_ref[...]).astype(BF16)
    z = jnp.dot(h, win_ref[...], preferred_element_type=F32)
    z = 0.5 * z * (1.0 + lax.erf(z * (2.0 ** -0.5)))
    sd = z.shape[1] // 2
    u = z[:, :sd]
    v = _rms(z[:, sd:], vn_ref[...]).astype(BF16)
    gw = sd // SGU_GROUPS
    rows = []
    for n in range(t // SGU_CHUNK):
        cols = []
        for g in range(SGU_GROUPS):
            vg = v[SGU_CHUNK * n:SGU_CHUNK * (n + 1), gw * g:gw * (g + 1)]
            cols.append(jnp.dot(ws_ref[g], vg, preferred_element_type=F32))
        rows.append(jnp.concatenate(cols, axis=1) + bs_ref[...])
    mixed = jnp.concatenate(rows, axis=0) if len(rows) > 1 else rows[0]
    y = jnp.dot((u * mixed).astype(BF16), wout_ref[...], preferred_element_type=F32)
    o_ref[...] = x + g1_ref[...] * y


def _sgu(x, sh, sc, g1, nw, w_in, vn, w_s, bs_full, w_out, tile):
    b, s, d = x.shape
    t = min(tile, s)
    vec = pl.BlockSpec((None, 1, d), lambda bi, i: (bi, 0, 0))
    row = pl.BlockSpec((None, t, d), lambda bi, i: (bi, i, 0))
    return pl.pallas_call(
        _sgu_kernel,
        out_shape=jax.ShapeDtypeStruct(x.shape, F32),
        grid=(b, s // t),
        in_specs=[row, vec, vec, vec, _const_spec(nw.shape), _const_spec(w_in.shape),
                  _const_spec(vn.shape), _const_spec(w_s.shape), _const_spec(bs_full.shape),
                  _const_spec(w_out.shape)],
        out_specs=row,
        compiler_params=_cparams(2),
        name="sgu",
    )(x, sh, sc, g1, nw, w_in, vn, w_s, bs_full, w_out)


def _ffn_kernel(xp_ref, x_ref, xn_ref, sh_ref, sc_ref, g2_ref, nw_ref, wup_ref, cw_ref, cb_ref,
                wdn_ref, fn_ref, o_ref, z_scr, *, final_norm):
    i = pl.program_id(1)
    last = pl.num_programs(1) - 1
    x = x_ref[...]
    t = x.shape[0]
    xa = jnp.concatenate([xp_ref[...], x, xn_ref[...]], axis=0)
    h = _rms(xa, nw_ref[...]) * (1.0 + sc_ref[...]) + sh_ref[...]
    r = lax.broadcasted_iota(jnp.int32, (t + 2 * CONV_HALO, 1), 0)
    outside = ((r < CONV_HALO) & (i == 0)) | ((r >= t + CONV_HALO) & (i == last))
    h = jnp.where(outside, 0.0, h).astype(BF16)
    z_scr[...] = jnp.dot(h, wup_ref[...], preferred_element_type=F32)
    cw = cw_ref[...]
    zc = (cw[0:1] * z_scr[pl.ds(CONV_HALO - 1, t), :] + cw[1:2] * z_scr[pl.ds(CONV_HALO, t), :]
          + cw[2:3] * z_scr[pl.ds(CONV_HALO + 1, t), :] + cb_ref[...])
    f = zc.shape[1] // 2
    g = zc[:, :f]
    act = (g * _sigmoid(g) * zc[:, f:]).astype(BF16)
    y = jnp.dot(act, wdn_ref[...], preferred_element_type=F32)
    out = x + g2_ref[...] * y
    if final_norm:
        out = _rms(out, fn_ref[...])
    o_ref[...] = out


def _ffn(x, sh, sc, g2, nw, w_up, conv_w, conv_b, w_dn, fn, final_norm, tile):
    b, s, d = x.shape
    t = min(tile, s)
    tb = t // CONV_HALO
    nhb = s // CONV_HALO
    vec = pl.BlockSpec((None, 1, d), lambda bi, i: (bi, 0, 0))
    row = pl.BlockSpec((None, t, d), lambda bi, i: (bi, i, 0))
    prev = pl.BlockSpec((None, CONV_HALO, d), lambda bi, i: (bi, jnp.maximum(i * tb - 1, 0), 0))
    nxt = pl.BlockSpec((None, CONV_HALO, d), lambda bi, i: (bi, jnp.minimum((i + 1) * tb, nhb - 1), 0))
    return pl.pallas_call(
        functools.partial(_ffn_kernel, final_norm=final_norm),
        out_shape=jax.ShapeDtypeStruct(x.shape, F32),
        grid=(b, s // t),
        in_specs=[prev, row, nxt, vec, vec, vec, _const_spec(nw.shape), _const_spec(w_up.shape),
                  _const_spec(conv_w.shape), _const_spec(conv_b.shape), _const_spec(w_dn.shape),
                  _const_spec(fn.shape)],
        out_specs=row,
        scratch_shapes=[pltpu.VMEM((t + 2 * CONV_HALO, w_up.shape[1]), F32)],
        compiler_params=_cparams(2),
        name="ffn",
    )(x, x, x, sh, sc, g2, nw, w_up, conv_w, conv_b, w_dn, fn)


def _pair_swap_vec(g):
    return g.reshape(-1, 2)[:, ::-1].reshape(-1)


def _rope_tables(rows):
    n_freq = HEAD // 4
    inv = ROPE_THETA ** (-jnp.arange(n_freq, dtype=F32) / n_freq)
    row = jnp.repeat(jnp.arange(rows, dtype=F32), GRID_COLS)
    col = jnp.tile(jnp.arange(GRID_COLS, dtype=F32), rows)
    ang = jnp.concatenate([row[:, None] * inv, col[:, None] * inv], axis=-1)
    expand = lambda t: jnp.tile(jnp.repeat(t, 2, axis=-1), (1, N_DIFF))
    return expand(jnp.cos(ang)), expand(jnp.sin(ang))


def kernel(x, c, ctx, c_ctx, ada_w, ada_b, mix_norm, ffn_norm, final_norm, attn_w_in, attn_w_out,
           diff_lq1, diff_lk1, diff_lq2, diff_lk2, diff_subln, gqa_q_norm, gqa_k_norm, sgu_w_in,
           sgu_v_norm, sgu_w_s, sgu_b_s, sgu_w_out, ffn_w_up, ffn_conv_w, ffn_conv_b, ffn_w_down):
    b, s, d = x.shape
    n_ctx = ctx.shape[1]
    depth = ada_w.shape[0]
    last_attn = (depth - 1) // 2 * 2

    rows_pad = -(-(b + 1) // SUBLANES) * SUBLANES
    cvec = jnp.zeros((rows_pad, d), F32).at[:b].set(c).at[b].set(c_ctx)
    mods = _mod_vectors(cvec, ada_w, ada_b)

    cos_lat, sin_lat = _rope_tables(s // GRID_COLS)
    cos_ctx = jnp.ones((n_ctx, 4 * HEAD), F32)
    sin_ctx = jnp.zeros((n_ctx, 4 * HEAD), F32)
    eye = jnp.repeat(jnp.repeat(jnp.eye(N_GQ, dtype=BF16), HEAD, axis=0), HEAD, axis=1)
    fn = final_norm.reshape(1, d)

    def split_mod(m):
        return [m[:, k * d:(k + 1) * d].reshape(b, 1, d) for k in range(N_MOD)]

    for l in range(depth):
        i = l // 2
        is_attn = l % 2 == 0
        update_ctx = l < last_attn
        sh1, sc1, g1, sh2, sc2, g2 = split_mod(mods[l, :b])
        need_ctx = is_attn or update_ctx
        if need_ctx:
            csh1, csc1, cg1, csh2, csc2, cg2 = split_mod(jnp.broadcast_to(mods[l, b], (b, N_MOD * d)))
        nw1 = mix_norm[l].reshape(1, d)
        nw2 = ffn_norm[l].reshape(1, d)

        if is_attn:
            lam_init = 0.8 - 0.6 * math.exp(-0.3 * l)
            w = attn_w_in[i]
            w_qkv = w.astype(BF16)
            gq = jnp.tile(gqa_q_norm[i], N_GQ).reshape(1, -1)
            gqs = jnp.tile(_pair_swap_vec(gqa_q_norm[i]), N_GQ).reshape(1, -1)
            gk = jnp.tile(gqa_k_norm[i], N_GKV).reshape(1, -1)
            gks = jnp.tile(_pair_swap_vec(gqa_k_norm[i]), N_GKV).reshape(1, -1)
            w_out = attn_w_out[i].astype(BF16)
            vecs = [v[i].reshape(1, -1) for v in (diff_lq1, diff_lk1, diff_lq2, diff_lk2, diff_subln)]

            lat = _attn_proj(x, sh1, sc1, nw1, w_qkv, cos_lat, sin_lat, gq, gqs, gk, gks, eye, PROJ_TILE)
            cpr = _attn_proj(ctx, csh1, csc1, nw1, w_qkv, cos_ctx, sin_ctx, gq, gqs, gk, gks, eye, n_ctx)
            kn = jnp.maximum(jnp.max(lat[6], axis=1), jnp.max(cpr[6], axis=1))[:, 0]
            per_head = lambda v: jnp.broadcast_to(v.reshape(b, -1, HEAD)[:, :, :1, None],
                                                  (b, v.shape[1] // HEAD, 1, 2 * HEAD))
            knd, kng = per_head(kn[:, :2 * QK_A]), per_head(kn[:, 2 * QK_A:])
            od = _flash(lat[0], knd, cpr[1], cpr[2], lat[1], lat[2], 1, FLASH_COLS, F32)
            og = _flash(lat[3], kng, cpr[4], cpr[5], lat[4], lat[5], GQ_GROUP, FLASH_COLS // GQ_GROUP, BF16)
            x_new = _attn_out(od, og, *vecs, w_out, x, g1, lam_init, ATTN_OUT_TILE)
            if update_ctx:
                cod = _flash(cpr[0], knd, cpr[1], cpr[2], None, None, 1, FLASH_COLS, F32)
                cog = _flash(cpr[3], kng, cpr[4], cpr[5], None, None, GQ_GROUP, FLASH_COLS // GQ_GROUP, BF16)
                ctx = _attn_out(cod, cog, *vecs, w_out, ctx, cg1, lam_init, ATTN_OUT_TILE)
            x = x_new
        else:
            w_in = sgu_w_in[i].astype(BF16)
            w_s = sgu_w_s[i].astype(BF16)
            w_out = sgu_w_out[i].astype(BF16)
            vn = sgu_v_norm[i].reshape(1, -1)
            gw = sgu_w_out.shape[1] // SGU_GROUPS
            bs_full = jnp.repeat(sgu_b_s[i].T, gw, axis=1)
            x_new = _sgu(x, sh1, sc1, g1, nw1, w_in, vn, w_s, bs_full, w_out, SGU_TILE)
            if update_ctx:
                ctx = _sgu(ctx, csh1, csc1, cg1, nw1, w_in, vn, w_s, bs_full, w_out, SGU_TILE)
            x = x_new

        w_up = ffn_w_up[l].astype(BF16)
        w_dn = ffn_w_down[l].astype(BF16)
        cb = ffn_conv_b[l].reshape(1, -1)
        x = _ffn(x, sh2, sc2, g2, nw2, w_up, ffn_conv_w[l], cb, w_dn, fn, l == depth - 1, FFN_TILE)
        if update_ctx:
            ctx = _ffn(ctx, csh2, csc2, cg2, nw2, w_up, ffn_conv_w[l], cb, w_dn, fn, False, FFN_TILE)
    return x
```

```python
import functools
import math

import jax
import jax.numpy as jnp
from jax import lax
from jax.experimental import pallas as pl
from jax.experimental.pallas import tpu as pltpu

F32 = jnp.float32
BF16 = jnp.bfloat16

EPS = 1e-6
HEAD = 64
N_DIFF = 4
DIFF_V = 2 * HEAD
N_GQ = 8
N_GKV = 2
GQ_GROUP = N_GQ // N_GKV
ROPE_THETA = 10000.0
GRID_COLS = 64
N_MOD = 6
SGU_CHUNK = 128
SGU_GROUPS = 4
SUBLANES = 8
CONV_HALO = SUBLANES
STREAM_ROWS_PER_PUSH = 128
V_EXT = 2 * SUBLANES
MOD_TILE_N = 1536
PROJ_TILE = 512
FLASH_COLS = 4096
ATTN_OUT_TILE = 1024
SGU_TILE = 512
FFN_TILE = 512
QK_SCALE = HEAD ** -0.5 * math.log2(math.e)
VMEM_LIMIT = 56 * 1024 * 1024
BOUND_SLACK = 1.0 + 2.0 ** -6
MIN_DENOM = 2.0 ** -80

QK_A = N_DIFF * HEAD
C_Q1, C_Q2, C_K1, C_K2 = 0, QK_A, 2 * QK_A, 3 * QK_A
C_VA = 4 * QK_A
C_QB = C_VA + N_DIFF * DIFF_V
C_KB = C_QB + N_GQ * HEAD
C_VB = C_KB + N_GKV * HEAD
C_END = C_VB + N_GKV * HEAD


def _cparams(n_axes):
    return pltpu.CompilerParams(dimension_semantics=("arbitrary",) * n_axes,
                                vmem_limit_bytes=VMEM_LIMIT)


def _const_spec(shape):
    nd = len(shape)
    return pl.BlockSpec(shape, lambda *_: (0,) * nd, pipeline_mode=pl.Buffered(1))


def _rms(xf, g):
    ms = jnp.mean(xf * xf, axis=-1, keepdims=True)
    return xf * lax.rsqrt(ms + EPS) * g


def _sigmoid(x):
    return 1.0 / (1.0 + jnp.exp(-x))


def _mod_kernel(c_ref, w_ref, b_ref, o_ref):
    c = c_ref[...]
    s = c * _sigmoid(c)
    o_ref[...] = jnp.dot(s, w_ref[...], precision=lax.Precision.HIGHEST,
                         preferred_element_type=F32) + b_ref[...]


def _mod_vectors(cvec, ada_w, ada_b):
    depth, d, n = ada_w.shape
    rows = cvec.shape[0]
    tn = MOD_TILE_N
    return pl.pallas_call(
        _mod_kernel,
        out_shape=jax.ShapeDtypeStruct((depth, rows, n), F32),
        grid=(depth, n // tn),
        in_specs=[pl.BlockSpec((rows, d), lambda l, j: (0, 0)),
                  pl.BlockSpec((None, d, tn), lambda l, j: (l, 0, j)),
                  pl.BlockSpec((None, 1, tn), lambda l, j: (l, 0, j))],
        out_specs=pl.BlockSpec((None, rows, tn), lambda l, j: (l, 0, j)),
        compiler_params=_cparams(2),
        name="mod_vectors",
    )(cvec, ada_w, ada_b.reshape(depth, 1, n))


def _group_sumsq(a, e):
    sq = a * a
    hi = sq.astype(BF16)
    lo = (sq - hi.astype(F32)).astype(BF16)
    return (jnp.dot(hi, e, preferred_element_type=F32) + jnp.dot(lo, e, preferred_element_type=F32))


def _pair_partner(a):
    n = a.shape[1]
    even = lax.broadcasted_iota(jnp.int32, (1, n), 1) % 2 == 0
    return jnp.where(even, -pltpu.roll(a, n - 1, axis=1), pltpu.roll(a, 1, axis=1))


def _attn_proj_kernel(x_ref, sh_ref, sc_ref, nw_ref, w_ref, cos_ref, sin_ref,
                      gq_ref, gqs_ref, gk_ref, gks_ref, e_ref,
                      qd_ref, kd_ref, va_ref, qg_ref, kg_ref, vg_ref, kn_ref):
    x = x_ref[...]
    h = (_rms(x, nw_ref[...]) * (1.0 + sc_ref[...]) + sh_ref[...]).astype(BF16)
    a = jnp.dot(h, w_ref[...], preferred_element_type=F32)
    cos = cos_ref[...]
    sin = sin_ref[...]
    scale = QK_SCALE

    for j in range(4):
        aj = a[:, C_Q1 + QK_A * j:C_Q1 + QK_A * (j + 1)]
        r = aj * cos + _pair_partner(aj) * sin
        if j < 2:
            rt = (r * scale).T.astype(BF16)
            for hh in range(N_DIFF):
                qd_ref[j * N_DIFF + hh] = rt[HEAD * hh:HEAD * (hh + 1), :]
        else:
            r = r.astype(BF16)
            for hh in range(N_DIFF):
                kd_ref[(j - 2) * N_DIFF + hh] = r[:, HEAD * hh:HEAD * (hh + 1)]
            ssq = _group_sumsq(r.astype(F32), e_ref[:QK_A, :QK_A])
            kn_ref[:, QK_A * (j - 2):QK_A * (j - 1)] = jnp.max(ssq, axis=0, keepdims=True)
    t = x.shape[0]
    ext = (lax.broadcasted_iota(jnp.int32, (V_EXT, t), 0) == 0).astype(BF16)
    for hh in range(N_DIFF):
        va_ref[hh, 0, :DIFF_V, :] = a[:, C_VA + DIFF_V * hh:C_VA + DIFF_V * (hh + 1)].T.astype(BF16)
        va_ref[hh, 0, DIFF_V:, :] = ext

    e = e_ref[...]
    cos2 = jnp.concatenate([cos, cos], axis=1)
    sin2 = jnp.concatenate([sin, sin], axis=1)
    qb = a[:, C_QB:C_KB]
    rq = lax.rsqrt(_group_sumsq(qb, e) * (1.0 / HEAD) + EPS)
    qt = (rq * (qb * (gq_ref[...] * cos2) + _pair_partner(qb) * (gqs_ref[...] * sin2)) * scale).T.astype(BF16)
    for hh in range(N_GQ):
        qg_ref[hh] = qt[HEAD * hh:HEAD * (hh + 1), :]
    nk = N_GKV * HEAD
    kb = a[:, C_KB:C_VB]
    rk = lax.rsqrt(_group_sumsq(kb, e[:nk, :nk]) * (1.0 / HEAD) + EPS)
    k = (rk * (kb * (gk_ref[...] * cos[:, :nk]) + _pair_partner(kb) * (gks_ref[...] * sin[:, :nk]))).astype(BF16)
    kn_ref[:, 2 * QK_A:] = jnp.max(_group_sumsq(k.astype(F32), e[:nk, :nk]), axis=0, keepdims=True)
    vbt = a[:, C_VB:C_END].T.astype(BF16)
    for hh in range(N_GKV):
        kg_ref[hh] = k[:, HEAD * hh:HEAD * (hh + 1)]
        vg_ref[hh, 0, :HEAD, :] = vbt[HEAD * hh:HEAD * (hh + 1), :]
        vg_ref[hh, 0, HEAD:, :] = ext


def _attn_proj(x, sh, sc, nw, w_qkv, cos, sin, gq, gqs, gk, gks, e, tile):
    b, s, d = x.shape
    t = min(tile, s)
    hd = lambda n, w: jax.ShapeDtypeStruct((b, n, s, w), BF16)
    hspec = lambda n, w: pl.BlockSpec((None, n, t, w), lambda bi, i: (bi, 0, i, 0))
    qd = lambda n: jax.ShapeDtypeStruct((b, n, HEAD, s), BF16)
    qspec = lambda n: pl.BlockSpec((None, n, HEAD, t), lambda bi, i: (bi, 0, 0, i))
    vd = lambda n, w: jax.ShapeDtypeStruct((b, n, s // t, w + V_EXT, t), BF16)
    vspec = lambda n, w: pl.BlockSpec((None, n, 1, w + V_EXT, t), lambda bi, i: (bi, 0, i, 0, 0))
    vec = pl.BlockSpec((None, 1, d), lambda bi, i: (bi, 0, 0))
    n_kn = (2 * N_DIFF + N_GKV) * HEAD
    return pl.pallas_call(
        _attn_proj_kernel,
        out_shape=(qd(2 * N_DIFF), hd(2 * N_DIFF, HEAD), vd(N_DIFF, DIFF_V),
                   qd(N_GQ), hd(N_GKV, HEAD), vd(N_GKV, HEAD),
                   jax.ShapeDtypeStruct((b, s // t, 1, n_kn), F32)),
        grid=(b, s // t),
        in_specs=[pl.BlockSpec((None, t, d), lambda bi, i: (bi, i, 0)), vec, vec,
                  _const_spec((1, d)), _const_spec(w_qkv.shape),
                  pl.BlockSpec((t, 4 * HEAD), lambda bi, i: (i, 0)),
                  pl.BlockSpec((t, 4 * HEAD), lambda bi, i: (i, 0)),
                  _const_spec(gq.shape), _const_spec(gqs.shape),
                  _const_spec(gk.shape), _const_spec(gks.shape), _const_spec(e.shape)],
        out_specs=(qspec(2 * N_DIFF), hspec(2 * N_DIFF, HEAD), vspec(N_DIFF, DIFF_V),
                   qspec(N_GQ), hspec(N_GKV, HEAD), vspec(N_GKV, HEAD),
                   pl.BlockSpec((None, None, 1, n_kn), lambda bi, i: (bi, i, 0, 0))),
        compiler_params=_cparams(2),
        name="attn_proj",
    )(x, sh, sc, nw, w_qkv, cos, sin, gq, gqs, gk, gks, e)


def _flash_kernel(*refs, group, tq, dv, n_lat, unroll):
    if n_lat:
        q_ref, kn_ref, kc_ref, vc_ref, kl_ref, vl_ref, o_ref, m_scr, acc_scr, st0, st1, mx0, mx1 = refs
    else:
        q_ref, kn_ref, kc_ref, vc_ref, o_ref, m_scr, acc_scr = refs
    cols = group * tq
    qt = q_ref[0] if group == 1 else jnp.concatenate([q_ref[g] for g in range(group)], axis=1)

    def finalize():
        acc = acc_scr[...]
        o = (acc[:dv] / acc[dv:dv + 1]).T
        if group > 1:
            o = jnp.concatenate([o[g * tq:(g + 1) * tq] for g in range(group)], axis=1)
        o_ref[...] = o.astype(o_ref.dtype)

    qf = qt.astype(F32)
    bound = jnp.sqrt(jnp.sum(qf * qf, axis=0, keepdims=True) * kn_ref[:, :1]) * BOUND_SLACK
    acc_scr[...] = jnp.zeros(acc_scr.shape, F32)
    vpu_denominator = dv + V_EXT > STREAM_ROWS_PER_PUSH
    if vpu_denominator:
        m_scr[...] = jnp.zeros(m_scr.shape, F32)

    def add(k, vt):
        st = jnp.dot(k, qt, preferred_element_type=F32)
        p = jnp.exp2(st - bound)
        if vpu_denominator:
            m_scr[...] += jnp.sum(p, axis=0, keepdims=True)
            acc_scr[:dv, :] += jnp.dot(vt[:dv], p.astype(BF16), preferred_element_type=F32)
        else:
            acc_scr[...] += jnp.dot(vt, p.astype(BF16), preferred_element_type=F32)

    add(kc_ref[...], vc_ref[...])
    if n_lat:
        tk = vl_ref.shape[2]

        def fast_body(jj, carry):
            for u in range(unroll):
                j = unroll * jj + u
                start = pl.multiple_of(j * tk, tk)
                add(kl_ref[pl.ds(start, tk), :], vl_ref[j])
            return carry

        lax.fori_loop(0, n_lat // unroll, fast_body, 0)
    if vpu_denominator:
        acc_scr[dv:dv + 1, :] = m_scr[...]
    ok = jnp.min(acc_scr[dv:dv + 1, :]) > MIN_DENOM

    def scores(k):
        st = jnp.dot(k, qt, preferred_element_type=F32)
        return st, jnp.max(st, axis=0, keepdims=True)

    def consume(st, mx, vt):
        m_old = m_scr[...]
        m_new = jnp.maximum(m_old, mx)
        alpha = jnp.exp2(m_old - m_new)
        pt = jnp.exp2(st - m_new).astype(BF16)
        acc_scr[...] = alpha * acc_scr[...] + jnp.dot(vt, pt, preferred_element_type=F32)
        m_scr[...] = m_new

    def exact():
        m_scr[...] = jnp.full(m_scr.shape, -jnp.inf, F32)
        acc_scr[...] = jnp.zeros(acc_scr.shape, F32)
        consume(*scores(kc_ref[...]), vc_ref[...])
        if n_lat:
            def produce(j, st_ref, mx_ref):
                start = pl.multiple_of(j * tk, tk)
                st_ref[...], mx_ref[...] = scores(kl_ref[pl.ds(start, tk), :])

            bufs = ((st0, mx0), (st1, mx1))

            def run(j, produce_next):
                for u in range(len(bufs)):
                    if u < len(bufs) - 1 or produce_next:
                        produce(j + u + 1, *bufs[(u + 1) % 2])
                    st_ref, mx_ref = bufs[u % 2]
                    consume(st_ref[...], mx_ref[...], vl_ref[j + u])

            produce(0, st0, mx0)

            def body(jj, carry):
                run(len(bufs) * jj, True)
                return carry

            lax.fori_loop(0, n_lat // len(bufs) - 1, body, 0)
            run(n_lat - len(bufs), False)

    @pl.when(jnp.logical_not(ok))
    def _():
        exact()

    finalize()


def _flash(q, kn, kc, vc, kl, vl, group, tq, out_dtype):
    b, hq, _, s = q.shape
    hk, sc = kc.shape[1], kc.shape[2]
    hv, dvx = vc.shape[1], vc.shape[3]
    dv = dvx - V_EXT
    tq = min(tq, s)
    cols = group * tq
    n_lat = 0 if kl is None else vl.shape[2]
    in_specs = [pl.BlockSpec((None, group, HEAD, tq), lambda bi, h, i: (bi, h, 0, i)),
                pl.BlockSpec((None, None, 1, 2 * HEAD), lambda bi, h, i: (bi, h, 0, 0)),
                pl.BlockSpec((None, None, sc, HEAD), lambda bi, h, i: (bi, h, 0, 0)),
                pl.BlockSpec((None, None, None, dvx, sc), lambda bi, h, i: (bi, h % hv, 0, 0, 0))]
    args = [q, kn, kc, vc]
    scratch = [pltpu.VMEM((1, cols), F32), pltpu.VMEM((dvx, cols), F32)]
    if n_lat:
        assert n_lat % 2 == 0, "latent key chunks alternate between two score buffers"
        sl, tk = kl.shape[2], vl.shape[4]
        in_specs += [pl.BlockSpec((None, None, sl, HEAD), lambda bi, h, i: (bi, h, 0, 0)),
                     pl.BlockSpec((None, None, n_lat, dvx, tk), lambda bi, h, i: (bi, h % hv, 0, 0, 0))]
        args += [kl, vl]
        scratch += [pltpu.VMEM((tk, cols), F32), pltpu.VMEM((tk, cols), F32),
                    pltpu.VMEM((1, cols), F32), pltpu.VMEM((1, cols), F32)]
    unroll = next(u for u in (8, 4, 2) if n_lat % u == 0)
    kern = functools.partial(_flash_kernel, group=group, tq=tq, dv=dv, n_lat=n_lat, unroll=unroll)
    return pl.pallas_call(
        kern,
        out_shape=jax.ShapeDtypeStruct((b, s, hq * dv), out_dtype),
        grid=(b, hk, s // tq),
        in_specs=in_specs,
        out_specs=pl.BlockSpec((None, tq, group * dv), lambda bi, h, i: (bi, i, h)),
        scratch_shapes=scratch,
        compiler_params=_cparams(3),
        name="flash_g%d_l%d" % (group, n_lat),
    )(*args)


def _attn_out_kernel(od_ref, og_ref, lq1_ref, lk1_ref, lq2_ref, lk2_ref, subln_ref, w_ref,
                     x_ref, g1_ref, o_ref, *, lam_init):
    lam = (jnp.exp(jnp.sum(lq1_ref[...] * lk1_ref[...], axis=-1, keepdims=True))
           - jnp.exp(jnp.sum(lq2_ref[...] * lk2_ref[...], axis=-1, keepdims=True)) + lam_init)
    od = od_ref[...]
    half = N_DIFF * DIFF_V
    parts = []
    for hh in range(N_DIFF):
        dh = od[:, DIFF_V * hh:DIFF_V * (hh + 1)] - lam * od[:, half + DIFF_V * hh:half + DIFF_V * (hh + 1)]
        parts.append(_rms(dh, subln_ref[...]) * (1.0 - lam_init))
    att = jnp.concatenate([p.astype(BF16) for p in parts] + [og_ref[...]], axis=1)
    y = jnp.dot(att, w_ref[...], preferred_element_type=F32)
    o_ref[...] = x_ref[...] + g1_ref[...] * y


def _attn_out(od, og, lq1, lk1, lq2, lk2, subln, w_out, x, g1, lam_init, tile):
    b, s, d = x.shape
    t = min(tile, s)
    vec = pl.BlockSpec((None, 1, d), lambda bi, i: (bi, 0, 0))
    row = lambda a: pl.BlockSpec((None, t, a.shape[2]), lambda bi, i: (bi, i, 0))
    return pl.pallas_call(
        functools.partial(_attn_out_kernel, lam_init=lam_init),
        out_shape=jax.ShapeDtypeStruct(x.shape, F32),
        grid=(b, s // t),
        in_specs=[row(od), row(og), _const_spec(lq1.shape), _const_spec(lk1.shape),
                  _const_spec(lq2.shape), _const_spec(lk2.shape), _const_spec(subln.shape),
                  _const_spec(w_out.shape), row(x), vec],
        out_specs=row(x),
        compiler_params=_cparams(2),
        name="attn_out",
    )(od, og, lq1, lk1, lq2, lk2, subln, w_out, x, g1)


def _sgu_kernel(x_ref, sh_ref, sc_ref, g1_ref, nw_ref, win_ref, vn_ref, ws_ref, bs_ref, wout_ref, o_ref):
    x = x_ref[...]
    t, d = x.shape
    h = (_rms(x, nw_ref[...]) * (1.0 + sc_ref[...]) + sh_ref[...]).astype(BF16)
    z = jnp.dot(h, win_ref[...], preferred_element_type=F32)
    z = 0.5 * z * (1.0 + lax.erf(z * (2.0 ** -0.5)))
    sd = z.shape[1] // 2
    u = z[:, :sd]
    v = _rms(z[:, sd:], vn_ref[...]).astype(BF16)
    gw = sd // SGU_GROUPS
    rows = []
    for n in range(t // SGU_CHUNK):
        cols = []
        for g in range(SGU_GROUPS):
            vg = v[SGU_CHUNK * n:SGU_CHUNK * (n + 1), gw * g:gw * (g + 1)]
            cols.append(jnp.dot(ws_ref[g], vg, preferred_element_type=F32))
        rows.append(jnp.concatenate(cols, axis=1) + bs_ref[...])
    mixed = jnp.concatenate(rows, axis=0) if len(rows) > 1 else rows[0]
    y = jnp.dot((u * mixed).astype(BF16), wout_ref[...], preferred_element_type=F32)
    o_ref[...] = x + g1_ref[...] * y


def _sgu(x, sh, sc, g1, nw, w_in, vn, w_s, bs_full, w_out, tile):
    b, s, d = x.shape
    t = min(tile, s)
    vec = pl.BlockSpec((None, 1, d), lambda bi, i: (bi, 0, 0))
    row = pl.BlockSpec((None, t, d), lambda bi, i: (bi, i, 0))
    return pl.pallas_call(
        _sgu_kernel,
        out_shape=jax.ShapeDtypeStruct(x.shape, F32),
        grid=(b, s // t),
        in_specs=[row, vec, vec, vec, _const_spec(nw.shape), _const_spec(w_in.shape),
                  _const_spec(vn.shape), _const_spec(w_s.shape), _const_spec(bs_full.shape),
                  _const_spec(w_out.shape)],
        out_specs=row,
        compiler_params=_cparams(2),
        name="sgu",
    )(x, sh, sc, g1, nw, w_in, vn, w_s, bs_full, w_out)


def _ffn_kernel(xp_ref, x_ref, xn_ref, sh_ref, sc_ref, g2_ref, nw_ref, wup_ref, cw_ref, cb_ref,
                wdn_ref, fn_ref, o_ref, *, final_norm):
    i = pl.program_id(1)
    last = pl.num_programs(1) - 1
    x = x_ref[...]
    t = x.shape[0]
    xa = jnp.concatenate([xp_ref[...], x, xn_ref[...]], axis=0)
    h = _rms(xa, nw_ref[...]) * (1.0 + sc_ref[...]) + sh_ref[...]
    r = lax.broadcasted_iota(jnp.int32, (t + 2 * CONV_HALO, 1), 0)
    outside = ((r < CONV_HALO) & (i == 0)) | ((r >= t + CONV_HALO) & (i == last))
    h = jnp.where(outside, 0.0, h).astype(BF16)
    z = jnp.dot(h, wup_ref[...], preferred_element_type=F32)
    rows = z.shape[0]
    cw = cw_ref[...]
    zc = (cw[0:1] * pltpu.roll(z, 1, axis=0)[CONV_HALO:CONV_HALO + t] + cw[1:2] * z[CONV_HALO:CONV_HALO + t]
          + cw[2:3] * pltpu.roll(z, rows - 1, axis=0)[CONV_HALO:CONV_HALO + t] + cb_ref[...])
    f = zc.shape[1] // 2
    g = zc[:, :f]
    act = (g * _sigmoid(g) * zc[:, f:]).astype(BF16)
    y = jnp.dot(act, wdn_ref[...], preferred_element_type=F32)
    out = x + g2_ref[...] * y
    if final_norm:
        out = _rms(out, fn_ref[...])
    o_ref[...] = out


def _ffn(x, sh, sc, g2, nw, w_up, conv_w, conv_b, w_dn, fn, final_norm, tile):
    b, s, d = x.shape
    t = min(tile, s)
    tb = t // CONV_HALO
    nhb = s // CONV_HALO
    vec = pl.BlockSpec((None, 1, d), lambda bi, i: (bi, 0, 0))
    row = pl.BlockSpec((None, t, d), lambda bi, i: (bi, i, 0))
    prev = pl.BlockSpec((None, CONV_HALO, d), lambda bi, i: (bi, jnp.maximum(i * tb - 1, 0), 0))
    nxt = pl.BlockSpec((None, CONV_HALO, d), lambda bi, i: (bi, jnp.minimum((i + 1) * tb, nhb - 1), 0))
    return pl.pallas_call(
        functools.partial(_ffn_kernel, final_norm=final_norm),
        out_shape=jax.ShapeDtypeStruct(x.shape, F32),
        grid=(b, s // t),
        in_specs=[prev, row, nxt, vec, vec, vec, _const_spec(nw.shape), _const_spec(w_up.shape),
                  _const_spec(conv_w.shape), _const_spec(conv_b.shape), _const_spec(w_dn.shape),
                  _const_spec(fn.shape)],
        out_specs=row,
        compiler_params=_cparams(2),
        name="ffn",
    )(x, x, x, sh, sc, g2, nw, w_up, conv_w, conv_b, w_dn, fn)


def _pair_swap_vec(g):
    return g.reshape(-1, 2)[:, ::-1].reshape(-1)


def _rope_tables(rows):
    n_freq = HEAD // 4
    inv = ROPE_THETA ** (-jnp.arange(n_freq, dtype=F32) / n_freq)
    row = jnp.repeat(jnp.arange(rows, dtype=F32), GRID_COLS)
    col = jnp.tile(jnp.arange(GRID_COLS, dtype=F32), rows)
    ang = jnp.concatenate([row[:, None] * inv, col[:, None] * inv], axis=-1)
    expand = lambda t: jnp.tile(jnp.repeat(t, 2, axis=-1), (1, N_DIFF))
    return expand(jnp.cos(ang)), expand(jnp.sin(ang))


def kernel(x, c, ctx, c_ctx, ada_w, ada_b, mix_norm, ffn_norm, final_norm, attn_w_in, attn_w_out,
           diff_lq1, diff_lk1, diff_lq2, diff_lk2, diff_subln, gqa_q_norm, gqa_k_norm, sgu_w_in,
           sgu_v_norm, sgu_w_s, sgu_b_s, sgu_w_out, ffn_w_up, ffn_conv_w, ffn_conv_b, ffn_w_down):
    b, s, d = x.shape
    n_ctx = ctx.shape[1]
    depth = ada_w.shape[0]
    last_attn = (depth - 1) // 2 * 2

    rows_pad = -(-(b + 1) // SUBLANES) * SUBLANES
    cvec = jnp.zeros((rows_pad, d), F32).at[:b].set(c).at[b].set(c_ctx)
    mods = _mod_vectors(cvec, ada_w, ada_b)

    cos_lat, sin_lat = _rope_tables(s // GRID_COLS)
    cos_ctx = jnp.ones((n_ctx, 4 * HEAD), F32)
    sin_ctx = jnp.zeros((n_ctx, 4 * HEAD), F32)
    eye = jnp.repeat(jnp.repeat(jnp.eye(N_GQ, dtype=BF16), HEAD, axis=0), HEAD, axis=1)
    fn = final_norm.reshape(1, d)

    def split_mod(m):
        return [m[:, k * d:(k + 1) * d].reshape(b, 1, d) for k in range(N_MOD)]

    for l in range(depth):
        i = l // 2
        is_attn = l % 2 == 0
        update_ctx = l < last_attn
        sh1, sc1, g1, sh2, sc2, g2 = split_mod(mods[l, :b])
        need_ctx = is_attn or update_ctx
        if need_ctx:
            csh1, csc1, cg1, csh2, csc2, cg2 = split_mod(jnp.broadcast_to(mods[l, b], (b, N_MOD * d)))
        nw1 = mix_norm[l].reshape(1, d)
        nw2 = ffn_norm[l].reshape(1, d)

        if is_attn:
            lam_init = 0.8 - 0.6 * math.exp(-0.3 * l)
            w = attn_w_in[i]
            w_qkv = w.astype(BF16)
            gq = jnp.tile(gqa_q_norm[i], N_GQ).reshape(1, -1)
            gqs = jnp.tile(_pair_swap_vec(gqa_q_norm[i]), N_GQ).reshape(1, -1)
            gk = jnp.tile(gqa_k_norm[i], N_GKV).reshape(1, -1)
            gks = jnp.tile(_pair_swap_vec(gqa_k_norm[i]), N_GKV).reshape(1, -1)
            w_out = attn_w_out[i].astype(BF16)
            vecs = [v[i].reshape(1, -1) for v in (diff_lq1, diff_lk1, diff_lq2, diff_lk2, diff_subln)]

            lat = _attn_proj(x, sh1, sc1, nw1, w_qkv, cos_lat, sin_lat, gq, gqs, gk, gks, eye, PROJ_TILE)
            cpr = _attn_proj(ctx, csh1, csc1, nw1, w_qkv, cos_ctx, sin_ctx, gq, gqs, gk, gks, eye, n_ctx)
            kn = jnp.maximum(jnp.max(lat[6], axis=1), jnp.max(cpr[6], axis=1))[:, 0]
            per_head = lambda v: jnp.broadcast_to(v.reshape(b, -1, HEAD)[:, :, :1, None],
                                                  (b, v.shape[1] // HEAD, 1, 2 * HEAD))
            knd, kng = per_head(kn[:, :2 * QK_A]), per_head(kn[:, 2 * QK_A:])
            od = _flash(lat[0], knd, cpr[1], cpr[2], lat[1], lat[2], 1, FLASH_COLS, F32)
            og = _flash(lat[3], kng, cpr[4], cpr[5], lat[4], lat[5], GQ_GROUP, FLASH_COLS // GQ_GROUP, BF16)
            x_new = _attn_out(od, og, *vecs, w_out, x, g1, lam_init, ATTN_OUT_TILE)
            if update_ctx:
                cod = _flash(cpr[0], knd, cpr[1], cpr[2], None, None, 1, FLASH_COLS, F32)
                cog = _flash(cpr[3], kng, cpr[4], cpr[5], None, None, GQ_GROUP, FLASH_COLS // GQ_GROUP, BF16)
                ctx = _attn_out(cod, cog, *vecs, w_out, ctx, cg1, lam_init, ATTN_OUT_TILE)
            x = x_new
        else:
            w_in = sgu_w_in[i].astype(BF16)
            w_s = sgu_w_s[i].astype(BF16)
            w_out = sgu_w_out[i].astype(BF16)
            vn = sgu_v_norm[i].reshape(1, -1)
            gw = sgu_w_out.shape[1] // SGU_GROUPS
            bs_full = jnp.repeat(sgu_b_s[i].T, gw, axis=1)
            x_new = _sgu(x, sh1, sc1, g1, nw1, w_in, vn, w_s, bs_full, w_out, SGU_TILE)
            if update_ctx:
                ctx = _sgu(ctx, csh1, csc1, cg1, nw1, w_in, vn, w_s, bs_full, w_out, SGU_TILE)
            x = x_new

        w_up = ffn_w_up[l].astype(BF16)
        w_dn = ffn_w_down[l].astype(BF16)
        cb = ffn_conv_b[l].reshape(1, -1)
        x = _ffn(x, sh2, sc2, g2, nw2, w_up, ffn_conv_w[l], cb, w_dn, fn, l == depth - 1, FFN_TILE)
        if update_ctx:
            ctx = _ffn(ctx, csh2, csc2, cg2, nw2, w_up, ffn_conv_w[l], cb, w_dn, fn, False, FFN_TILE)
    return x
```
